```python
import math
import jax, jax.numpy as jnp
from jax import lax
import numpy as np

D_MODEL = 1024
BATCH = 8
SEQ = 2048
DEPTH = 4
DEC_BATCH = 128
DEC_SEQ = 1
PAST_LEN = 2048
PAGE_SIZE = 128

N_AB = (DEPTH + 1) // 2
N_C = DEPTH // 2
H_A = 8
DH_A = 64
W_A = H_A * DH_A
LORA_W = 64
LORA_A = 64
LORA_G = 128
H_B = 8
DH_B = 64
W_B = H_B * DH_B
H_C = 8
DH_C = 64
W_C = H_C * 2 * DH_C
D_FF = -(-8 * D_MODEL // (3 * 256)) * 256
N_BUCKETS = 32
MAX_DISTANCE = 128
QBLOCK = 128
RMS_EPS = 1e-6
GN_EPS = 64e-5
NEG_INF = -1e30
F32 = jnp.float32

A_SIZES = (W_A, LORA_W, W_A, W_A, LORA_A, LORA_G)
B_SIZES = (W_B, W_B, W_B, H_B)
A_COLS = sum(A_SIZES)
IN_AB = A_COLS + sum(B_SIZES)
IN_C = 3 * W_C

kernel_name = 'hybrid_rwkv7_fox_diffattn_step'


def split_cols(z, sizes):
    return jnp.split(z, np.cumsum(sizes)[:-1].tolist(), axis=-1)


def rmsnorm(x, g):
    xf = x.astype(F32)
    y = xf * lax.rsqrt(jnp.mean(xf * xf, axis=-1, keepdims=True) + RMS_EPS)
    return (y * g.astype(F32)).astype(x.dtype)


def swiglu(x, wg, wu, wd):
    return (jax.nn.silu(x @ wg) * (x @ wu)) @ wd


def gather_pages(pool, page_table):
    rows = pool[page_table]
    return rows.reshape((page_table.shape[0], page_table.shape[1] * pool.shape[1]) + pool.shape[2:])


def t5_bucket(dist):
    n = jnp.maximum(dist, 0)
    max_exact = N_BUCKETS // 2
    nf = jnp.maximum(n, 1).astype(F32)
    large = max_exact + (jnp.log(nf / max_exact) / math.log(MAX_DISTANCE / max_exact)
                         * (N_BUCKETS - max_exact)).astype(jnp.int32)
    large = jnp.minimum(large, N_BUCKETS - 1)
    return jnp.where(n < max_exact, n, large)


def sweep_query_blocks(block_fn, t0, *q_arrays):
    t = q_arrays[0].shape[1]
    blk = QBLOCK if t % QBLOCK == 0 else t
    nb = t // blk

    def to_blocks(a):
        return jnp.moveaxis(a.reshape((a.shape[0], nb, blk) + a.shape[2:]), 1, 0)

    pos = t0 + jnp.arange(t, dtype=jnp.int32).reshape(nb, blk)
    out = lax.map(lambda args: block_fn(*args), (pos,) + tuple(to_blocks(a) for a in q_arrays))
    out = jnp.moveaxis(out, 0, 1)
    return out.reshape((out.shape[0], t) + out.shape[3:])


def rwkv7_mix(pa, pa_prev, s0, mu, w0, w_up, a0, a_up, g_up, k_k, k_a, r_k, gn_g, gn_b):
    b, t, _ = pa.shape
    shifted = jnp.concatenate([pa_prev[:, None, :], pa[:, :-1]], axis=1)
    z = pa + (shifted - pa) * mu
    r, wd, k, v, ad, gd = split_cols(z, A_SIZES)
    w = -jax.nn.softplus(-(w0 + jnp.tanh(wd) @ w_up).astype(F32)) - 0.5
    decay = jnp.exp(-jnp.exp(w))
    a = jax.nn.sigmoid(a0 + ad @ a_up)
    g = jax.nn.sigmoid(gd) @ g_up

    def hs(u):
        return u.reshape(b, t, H_A, DH_A).astype(F32)

    kk = hs(k * k_k)
    kk = kk / jnp.maximum(jnp.sqrt(jnp.sum(kk * kk, axis=-1, keepdims=True)), 1e-12)
    k = k * (1.0 + (a - 1.0) * k_a)
    r_h, k_h, v_h, a_h, w_h = hs(r), hs(k), hs(v), hs(a), hs(decay)

    def step(S, inp):
        r_t, w_t, k_t, v_t, kk_t, b_t = inp
        sa = jnp.einsum('bhij,bhj->bhi', S, -kk_t)
        S = S * w_t[:, :, None, :] + sa[..., None] * b_t[:, :, None, :] + v_t[..., None] * k_t[:, :, None, :]
        return S, jnp.einsum('bhij,bhj->bhi', S, r_t)

    xs = tuple(jnp.moveaxis(u, 1, 0) for u in (r_h, w_h, k_h, v_h, kk, kk * a_h))
    s_fin, y = lax.scan(step, s0.astype(F32), xs)
    y = jnp.moveaxis(y, 0, 1)
    mean = jnp.mean(y, axis=-1, keepdims=True)
    var = jnp.mean(jnp.square(y - mean), axis=-1, keepdims=True)
    y = ((y - mean) * lax.rsqrt(var + GN_EPS)).reshape(b, t, W_A) * gn_g.astype(F32) + gn_b.astype(F32)
    bonus = jnp.sum(r_h * k_h * r_k.astype(F32), axis=-1, keepdims=True) * v_h
    y = (y + bonus.reshape(b, t, W_A)) * g.astype(F32)
    return y.astype(pa.dtype), s_fin


def fox_mix(pb, k_past, v_past, logf_past, b_f):
    b, t, _ = pb.shape
    q, k, v, f = split_cols(pb, B_SIZES)
    q = q.reshape(b, t, H_B, DH_B)
    k = k.reshape(b, t, H_B, DH_B)
    v = v.reshape(b, t, H_B, DH_B)
    logf = jax.nn.log_sigmoid((f + b_f).astype(F32))
    t0 = k_past.shape[1]
    k_all = jnp.concatenate([k_past.astype(k.dtype), k], axis=1)
    v_all = jnp.concatenate([v_past.astype(v.dtype), v], axis=1)
    c = jnp.cumsum(jnp.concatenate([logf_past.astype(F32), logf], axis=1), axis=1)
    c_k = jnp.swapaxes(c, 1, 2)
    kpos = jnp.arange(t0 + t, dtype=jnp.int32)
    scale = DH_B ** -0.5

    def block(qpos, qb, cqb):
        s = jnp.einsum('bqhd,bkhd->bhqk', qb, k_all).astype(F32) * scale
        s = s + jnp.swapaxes(cqb, 1, 2)[..., None] - c_k[:, :, None, :]
        s = jnp.where(kpos[None, :] <= qpos[:, None], s, NEG_INF)
        p = jax.nn.softmax(s, axis=-1).astype(v_all.dtype)
        return jnp.einsum('bhqk,bkhd->bqhd', p, v_all)

    o = sweep_query_blocks(block, t0, q, c[:, t0:])
    return o.reshape(b, t, W_B), k, v, logf


def diff_mix(pc, k_past, v_past, lq1, lk1, lq2, lk2, subln_g, rel_bias, lambda_init):
    b, t, _ = pc.shape
    q, k, v = split_cols(pc, (W_C, W_C, W_C))
    q = q.reshape(b, t, H_C, 2, DH_C)
    k = k.reshape(b, t, H_C, 2 * DH_C)
    v = v.reshape(b, t, H_C, 2 * DH_C)
    t0 = k_past.shape[1]
    tk = t0 + t
    k_all = jnp.concatenate([k_past.astype(k.dtype), k], axis=1).reshape(b, tk, H_C, 2, DH_C)
    v_all = jnp.concatenate([v_past.astype(v.dtype), v], axis=1)
    lam = (jnp.exp(jnp.sum(lq1.astype(F32) * lk1.astype(F32)))
           - jnp.exp(jnp.sum(lq2.astype(F32) * lk2.astype(F32))) + lambda_init)
    kpos = jnp.arange(tk, dtype=jnp.int32)
    table = rel_bias.astype(F32)
    scale = DH_C ** -0.5

    def block(qpos, qb):
        s = jnp.einsum('bqhmd,bkhmd->bhmqk', qb, k_all).astype(F32) * scale
        bias = table[t5_bucket(qpos[:, None] - kpos[None, :])]
        bias = jnp.transpose(bias.reshape(qpos.shape[0], tk, H_C, 2), (2, 3, 0, 1))
        s = jnp.where(kpos[None, :] <= qpos[:, None], s + bias, NEG_INF)
        p = jax.nn.softmax(s, axis=-1)
        pd = (p[:, :, 0] - lam * p[:, :, 1]).astype(v_all.dtype)
        return jnp.einsum('bhqk,bkhe->bqhe', pd, v_all)

    o = sweep_query_blocks(block, t0, q)
    o = rmsnorm(o, subln_g) * (1.0 - lambda_init)
    return o.reshape(b, t, W_C), k, v


def trunk(x, past, p):
    fk, fv, fl, dk, dv, rs, sh = [], [], [], [], [], [], []
    for li in range(DEPTH):
        xn = rmsnorm(x, p['norm_mix'][li])
        j = li // 2
        if li % 2 == 0:
            w_in = p['w_in_ab'][j]
            proj = xn @ w_in
            pa, pb = proj[..., :A_COLS], proj[..., A_COLS:]
            pa_prev = past('state_shift', j) @ w_in[:, :A_COLS]
            ya, s_fin = rwkv7_mix(pa, pa_prev, past('state_rwkv', j), p['rwkv_mu'][j], p['rwkv_w0'][j],
                                  p['rwkv_w_up'][j], p['rwkv_a0'][j], p['rwkv_a_up'][j], p['rwkv_g_up'][j],
                                  p['rwkv_k_k'][j], p['rwkv_k_a'][j], p['rwkv_r_k'][j],
                                  p['rwkv_gn_g'][j], p['rwkv_gn_b'][j])
            yb, k_new, v_new, lf_new = fox_mix(pb, past('cache_fox_k', j), past('cache_fox_v', j),
                                               past('cache_fox_logf', j), p['fox_b_f'][j])
            h = jnp.concatenate([ya, yb], axis=-1) @ p['w_out_ab'][j]
            fk.append(k_new)
            fv.append(v_new)
            fl.append(lf_new)
            rs.append(s_fin)
            sh.append(xn[:, -1])
        else:
            pc = xn @ p['w_in_c'][j]
            yc, k_new, v_new = diff_mix(pc, past('cache_diff_k', j), past('cache_diff_v', j),
                                        p['diff_lq1'][j], p['diff_lk1'][j], p['diff_lq2'][j], p['diff_lk2'][j],
                                        p['diff_subln'][j], p['rel_bias'],
                                        0.8 - 0.6 * math.exp(-0.3 * li))
            h = yc @ p['w_out_c'][j]
            dk.append(k_new)
            dv.append(v_new)
        x = x + h
        x = x + swiglu(rmsnorm(x, p['norm_ffn'][li]), p['ffn_gate'][li], p['ffn_up'][li], p['ffn_down'][li])
    y = rmsnorm(x, p['norm_final'])
    return y, (jnp.stack(fk), jnp.stack(fv), jnp.stack(fl), jnp.stack(dk), jnp.stack(dv),
               jnp.stack(rs), jnp.stack(sh))


def setup_inputs(seed: int = 0) -> dict:
    key = jax.random.key(seed)
    ks = iter(jax.random.split(key, 64))

    def nrm(shape, scale=1.0):
        return jax.random.normal(next(ks), shape, F32) * scale

    def uni(shape, lo, hi):
        return jax.random.uniform(next(ks), shape, F32, lo, hi)

    n_pages = PAST_LEN // PAGE_SIZE
    n_used = DEC_BATCH * n_pages
    n_pool = n_used + max(1, n_used // 4)
    page_table = jax.random.permutation(next(ks), n_pool)[:n_used].reshape(DEC_BATCH, n_pages).astype(jnp.int32)
    return {
        'x_prompt': nrm((BATCH, SEQ, D_MODEL)),
        'x_sample': nrm((DEC_BATCH, DEC_SEQ, D_MODEL)),
        'cache_fox_k': nrm((N_AB, n_pool, PAGE_SIZE, H_B, DH_B)),
        'cache_fox_v': nrm((N_AB, n_pool, PAGE_SIZE, H_B, DH_B)),
        'cache_fox_logf': jax.nn.log_sigmoid(nrm((N_AB, n_pool, PAGE_SIZE, H_B)) + 2.5),
        'cache_diff_k': nrm((N_C, n_pool, PAGE_SIZE, H_C, 2 * DH_C)),
        'cache_diff_v': nrm((N_C, n_pool, PAGE_SIZE, H_C, 2 * DH_C)),
        'state_rwkv': nrm((N_AB, DEC_BATCH, H_A, DH_A, DH_A), 0.3),
        'state_shift': nrm((N_AB, DEC_BATCH, D_MODEL)),
        'page_table': page_table,
        'norm_mix': 1.0 + nrm((DEPTH, D_MODEL), 0.02),
        'norm_ffn': 1.0 + nrm((DEPTH, D_MODEL), 0.02),
        'norm_final': 1.0 + nrm((D_MODEL,), 0.02),
        'w_in_ab': nrm((N_AB, D_MODEL, IN_AB), D_MODEL ** -0.5),
        'w_out_ab': nrm((N_AB, W_A + W_B, D_MODEL), (W_A + W_B) ** -0.5),
        'rwkv_mu': uni((N_AB, A_COLS), 0.0, 1.0),
        'rwkv_w0': uni((N_AB, W_A), -6.0, -1.0),
        'rwkv_w_up': nrm((N_AB, LORA_W, W_A), 0.3 * LORA_W ** -0.5),
        'rwkv_a0': nrm((N_AB, W_A), 0.1),
        'rwkv_a_up': nrm((N_AB, LORA_A, W_A), LORA_A ** -0.5),
        'rwkv_g_up': nrm((N_AB, LORA_G, W_A), LORA_G ** -0.5),
        'rwkv_k_k': 0.85 + nrm((N_AB, W_A), 0.02),
        'rwkv_k_a': 1.0 + nrm((N_AB, W_A), 0.02),
        'rwkv_r_k': nrm((N_AB, H_A, DH_A), 0.1),
        'rwkv_gn_g': 1.0 + nrm((N_AB, W_A), 0.02),
        'rwkv_gn_b': nrm((N_AB, W_A), 0.02),
        'fox_b_f': uni((N_AB, H_B), 1.0, 4.0),
        'w_in_c': nrm((N_C, D_MODEL, IN_C), D_MODEL ** -0.5),
        'w_out_c': nrm((N_C, W_C, D_MODEL), W_C ** -0.5),
        'diff_lq1': nrm((N_C, DH_C), 0.1),
        'diff_lk1': nrm((N_C, DH_C), 0.1),
        'diff_lq2': nrm((N_C, DH_C), 0.1),
        'diff_lk2': nrm((N_C, DH_C), 0.1),
        'diff_subln': 1.0 + nrm((N_C, 2 * DH_C), 0.02),
        'rel_bias': nrm((N_BUCKETS, 2 * H_C), 0.5),
        'ffn_gate': nrm((DEPTH, D_MODEL, D_FF), D_MODEL ** -0.5),
        'ffn_up': nrm((DEPTH, D_MODEL, D_FF), D_MODEL ** -0.5),
        'ffn_down': nrm((DEPTH, D_FF, D_MODEL), D_FF ** -0.5),
    }


def reference(x_prompt, x_sample, cache_fox_k, cache_fox_v, cache_fox_logf, cache_diff_k, cache_diff_v,
              state_rwkv, state_shift, page_table, norm_mix, norm_ffn, norm_final, w_in_ab, w_out_ab,
              rwkv_mu, rwkv_w0, rwkv_w_up, rwkv_a0, rwkv_a_up, rwkv_g_up, rwkv_k_k, rwkv_k_a, rwkv_r_k,
              rwkv_gn_g, rwkv_gn_b, fox_b_f, w_in_c, w_out_c, diff_lq1, diff_lk1, diff_lq2, diff_lk2,
              diff_subln, rel_bias, ffn_gate, ffn_up, ffn_down):
    p = {'norm_mix': norm_mix, 'norm_ffn': norm_ffn, 'norm_final': norm_final,
         'w_in_ab': w_in_ab, 'w_out_ab': w_out_ab, 'rwkv_mu': rwkv_mu, 'rwkv_w0': rwkv_w0,
         'rwkv_w_up': rwkv_w_up, 'rwkv_a0': rwkv_a0, 'rwkv_a_up': rwkv_a_up, 'rwkv_g_up': rwkv_g_up,
         'rwkv_k_k': rwkv_k_k, 'rwkv_k_a': rwkv_k_a, 'rwkv_r_k': rwkv_r_k, 'rwkv_gn_g': rwkv_gn_g,
         'rwkv_gn_b': rwkv_gn_b, 'fox_b_f': fox_b_f, 'w_in_c': w_in_c, 'w_out_c': w_out_c,
         'diff_lq1': diff_lq1, 'diff_lk1': diff_lk1, 'diff_lq2': diff_lq2, 'diff_lk2': diff_lk2,
         'diff_subln': diff_subln, 'rel_bias': rel_bias, 'ffn_gate': ffn_gate, 'ffn_up': ffn_up,
         'ffn_down': ffn_down}
    bp = x_prompt.shape[0]
    dt = x_prompt.dtype

    def prompt_past(name, j):
        if name == 'state_shift':
            return jnp.zeros((bp, D_MODEL), dt)
        if name == 'state_rwkv':
            return jnp.zeros((bp, H_A, DH_A, DH_A), F32)
        if name == 'cache_fox_logf':
            return jnp.zeros((bp, 0, H_B), F32)
        if name in ('cache_fox_k', 'cache_fox_v'):
            return jnp.zeros((bp, 0, H_B, DH_B), dt)
        return jnp.zeros((bp, 0, H_C, 2 * DH_C), dt)

    paged = {'cache_fox_k': cache_fox_k, 'cache_fox_v': cache_fox_v, 'cache_fox_logf': cache_fox_logf,
             'cache_diff_k': cache_diff_k, 'cache_diff_v': cache_diff_v}
    per_seq = {'state_shift': state_shift, 'state_rwkv': state_rwkv}

    def sample_past(name, j):
        if name in per_seq:
            return per_seq[name][j]
        return gather_pages(paged[name][j], page_table)

    y_p, (fk_p, fv_p, fl_p, dk_p, dv_p, rs_p, sh_p) = trunk(x_prompt, prompt_past, p)
    y_s, (fk_s, fv_s, fl_s, dk_s, dv_s, rs_s, sh_s) = trunk(x_sample, sample_past, p)
    return (y_p, y_s, fk_p, fv_p, fl_p, dk_p, dv_p, rs_p, sh_p, fk_s, fv_s, fl_s, dk_s, dv_s, rs_s, sh_s)
```

```python
import functools
import math

import jax
import jax.numpy as jnp
import numpy as np
from jax import lax
from jax.experimental import pallas as pl
from jax.experimental.pallas import tpu as pltpu

F32 = jnp.float32
BF16 = jnp.bfloat16

D_MODEL = 1024
N_HEADS = 8
DH = 64
W_A = N_HEADS * DH
LORA_W = 64
LORA_A = 64
LORA_G = 128
A_COLS = 3 * W_A + LORA_W + LORA_A + LORA_G
W_C = N_HEADS * 2 * DH
N_BUCKETS = 32
MAX_DISTANCE = 128
RMS_EPS = 1e-6
GN_EPS = 64e-5
NEG_INF = -1e30
QK_SCALE = DH ** -0.5

LANE = 128
RWKV_CHUNK = 64
ATT_BLOCK = 256
VMEM_LIMIT = 48 * 1024 * 1024


def _cparams(*sem):
    return pltpu.CompilerParams(dimension_semantics=sem, vmem_limit_bytes=VMEM_LIMIT)


def _bdot(a, b):
    return jnp.dot(a.astype(BF16), b.astype(BF16), preferred_element_type=F32)


def _bdot_nt(a, b):
    return lax.dot_general(a.astype(BF16), b.astype(BF16), (((1,), (1,)), ((), ())),
                           preferred_element_type=F32)


def _bdot_tn(a, b):
    return lax.dot_general(a.astype(BF16), b.astype(BF16), (((0,), (0,)), ((), ())),
                           preferred_element_type=F32)


def _split3(x):
    hi = x.astype(BF16).astype(F32)
    r1 = x - hi
    mid = r1.astype(BF16).astype(F32)
    lo = (r1 - mid).astype(BF16).astype(F32)
    return hi, mid, lo


def _ones_dot_left(ones_mat, x):
    n = x.shape[1]
    stack = jnp.concatenate(_split3(x), axis=1).astype(BF16)
    out = jnp.dot(ones_mat.astype(BF16), stack, preferred_element_type=F32)
    return out[:, :n] + out[:, n:2 * n] + out[:, 2 * n:]


def _ones_dot_right(x, ones_mat):
    m = x.shape[0]
    stack = jnp.concatenate(_split3(x), axis=0).astype(BF16)
    out = jnp.dot(stack, ones_mat.astype(BF16), preferred_element_type=F32)
    return out[:m] + out[m:2 * m] + out[2 * m:]


def _sigmoid(x):
    return 1.0 / (1.0 + jnp.exp(-x))


def _log_sigmoid(x):
    return jnp.minimum(x, 0.0) - jnp.log(1.0 + jnp.exp(-jnp.abs(x)))


def _softplus(x):
    return jnp.maximum(x, 0.0) + jnp.log(1.0 + jnp.exp(-jnp.abs(x)))


def _rmsnorm(x, g):
    return x * lax.rsqrt(jnp.mean(x * x, axis=-1, keepdims=True) + RMS_EPS) * g


def _iota2(shape, dim):
    return lax.broadcasted_iota(jnp.int32, shape, dim)


def _row_tile(m):
    for t in (512, 256, 128):
        if m % t == 0:
            return t
    return m


def _norm_proj_kernel(*refs, n_out, with_norm, emit_xn):
    x_ref = refs[0]
    pos = 1
    if with_norm:
        g_ref = refs[pos]
        pos += 1
    w_refs = refs[pos:pos + n_out]
    pos += n_out
    out_refs = refs[pos:pos + n_out]
    pos += n_out
    x = x_ref[...]
    if with_norm:
        x = _rmsnorm(x, g_ref[...])
        if emit_xn:
            refs[pos][...] = x
    xb = x.astype(BF16)
    for w_ref, o_ref in zip(w_refs, out_refs):
        o_ref[...] = jnp.dot(xb, w_ref[...], preferred_element_type=F32)


def norm_proj(x, g, ws, emit_xn=False):
    m, d = x.shape
    tm = _row_tile(m)
    with_norm = g is not None
    ins = [x]
    in_specs = [pl.BlockSpec((tm, d), lambda i: (i, 0))]
    if with_norm:
        ins.append(g.reshape(1, d))
        in_specs.append(pl.BlockSpec((1, d), lambda i: (0, 0)))
    for w in ws:
        ins.append(w)
        in_specs.append(pl.BlockSpec(w.shape, lambda i: (0, 0)))
    out_shape = [jax.ShapeDtypeStruct((m, w.shape[1]), F32) for w in ws]
    out_specs = [pl.BlockSpec((tm, w.shape[1]), lambda i: (i, 0)) for w in ws]
    if emit_xn:
        out_shape.append(jax.ShapeDtypeStruct((m, d), F32))
        out_specs.append(pl.BlockSpec((tm, d), lambda i: (i, 0)))
    return pl.pallas_call(
        functools.partial(_norm_proj_kernel, n_out=len(ws), with_norm=with_norm, emit_xn=emit_xn),
        grid=(m // tm,), in_specs=in_specs, out_specs=out_specs, out_shape=out_shape,
        compiler_params=_cparams("parallel"), name="norm_proj")(*ins)


def _proj_residual_kernel(*refs, n_in):
    x_ref = refs[0]
    y_refs = refs[1:1 + n_in]
    w_refs = refs[1 + n_in:1 + 2 * n_in]
    o_ref = refs[1 + 2 * n_in]
    acc = x_ref[...]
    for y_ref, w_ref in zip(y_refs, w_refs):
        acc = acc + jnp.dot(y_ref[...].astype(BF16), w_ref[...], preferred_element_type=F32)
    o_ref[...] = acc


def proj_residual(x, ys, ws):
    m, d = x.shape
    tm = _row_tile(m)
    in_specs = [pl.BlockSpec((tm, d), lambda i: (i, 0))]
    in_specs += [pl.BlockSpec((tm, y.shape[1]), lambda i: (i, 0)) for y in ys]
    in_specs += [pl.BlockSpec(w.shape, lambda i: (0, 0)) for w in ws]
    return pl.pallas_call(
        functools.partial(_proj_residual_kernel, n_in=len(ys)),
        grid=(m // tm,), in_specs=in_specs, out_specs=pl.BlockSpec((tm, d), lambda i: (i, 0)),
        out_shape=jax.ShapeDtypeStruct((m, d), F32),
        compiler_params=_cparams("parallel"), name="proj_residual")(x, *ys, *ws)


FFN_TILE = 256


def _ffn_kernel(*refs, final_norm):
    if final_norm:
        x_ref, g_ref, wg_ref, wu_ref, wd_ref, gf_ref, o_ref, y_ref, xn_scr, acc_scr = refs
    else:
        x_ref, g_ref, wg_ref, wu_ref, wd_ref, o_ref, xn_scr, acc_scr = refs
    f = pl.program_id(1)

    @pl.when(f == 0)
    def _():
        xn_scr[...] = _rmsnorm(x_ref[...], g_ref[...]).astype(BF16)
        acc_scr[...] = jnp.zeros_like(acc_scr)

    xn = xn_scr[...]
    gate = jnp.dot(xn, wg_ref[...], preferred_element_type=F32)
    up = jnp.dot(xn, wu_ref[...], preferred_element_type=F32)
    h = (gate * _sigmoid(gate) * up).astype(BF16)
    acc_scr[...] += jnp.dot(h, wd_ref[...], preferred_element_type=F32)

    @pl.when(f == pl.num_programs(1) - 1)
    def _():
        out = x_ref[...] + acc_scr[...]
        o_ref[...] = out
        if final_norm:
            y_ref[...] = _rmsnorm(out, gf_ref[...])


def ffn_residual(x, g, wg, wu, wd, final_g=None):
    m, d = x.shape
    dff = wg.shape[1]
    tm = 1024 if m % 1024 == 0 else _row_tile(m)
    nf = dff // FFN_TILE
    final_norm = final_g is not None
    ins = [x, g.reshape(1, d), wg, wu, wd]
    in_specs = [pl.BlockSpec((tm, d), lambda i, f: (i, 0)),
                pl.BlockSpec((1, d), lambda i, f: (0, 0)),
                pl.BlockSpec((d, FFN_TILE), lambda i, f: (0, f)),
                pl.BlockSpec((d, FFN_TILE), lambda i, f: (0, f)),
                pl.BlockSpec((FFN_TILE, d), lambda i, f: (f, 0))]
    out_shape = [jax.ShapeDtypeStruct((m, d), F32)]
    out_specs = [pl.BlockSpec((tm, d), lambda i, f: (i, 0))]
    if final_norm:
        ins.append(final_g.reshape(1, d))
        in_specs.append(pl.BlockSpec((1, d), lambda i, f: (0, 0)))
        out_shape.append(jax.ShapeDtypeStruct((m, d), F32))
        out_specs.append(pl.BlockSpec((tm, d), lambda i, f: (i, 0)))
    res = pl.pallas_call(
        functools.partial(_ffn_kernel, final_norm=final_norm),
        grid=(m // tm, nf), in_specs=in_specs, out_specs=out_specs, out_shape=out_shape,
        scratch_shapes=[pltpu.VMEM((tm, d), BF16), pltpu.VMEM((tm, d), F32)],
        compiler_params=_cparams("parallel", "arbitrary"), name="ffn")(*ins)
    return res if final_norm else res[0]


def _head_ones():
    r = _iota2((W_A, W_A), 0) // DH
    c = _iota2((W_A, W_A), 1) // DH
    return (r == c).astype(F32)


def _rwkv_pre_kernel(*refs, seq_blocks, has_prev):
    if has_prev:
        (pa_ref, prev_ref, mu_ref, wlora_ref, gup_ref, w0_ref, a0_ref, kk_ref, ka_ref,
         r_out, lw_out, k_out, v_out, kkn_out, a_out, g_out) = refs
    else:
        (pa_ref, mu_ref, wlora_ref, gup_ref, w0_ref, a0_ref, kk_ref, ka_ref,
         r_out, lw_out, k_out, v_out, kkn_out, a_out, g_out, carry) = refs
    pa = pa_ref[...]
    tm = pa.shape[0]
    if has_prev:
        shifted = prev_ref[...]
    else:
        i = pl.program_id(0)
        first = (i % seq_blocks) == 0
        prev_row = jnp.where(first, 0.0, carry[...])
        rolled = pltpu.roll(pa, 1, 0)
        shifted = jnp.where(_iota2(pa.shape, 0) == 0, prev_row, rolled)
        carry[...] = pa[tm - 1:tm, :]
    z = pa + (shifted - pa) * mu_ref[...]
    r = z[:, 0:W_A]
    k = z[:, W_A:2 * W_A]
    v = z[:, 2 * W_A:3 * W_A]
    lo = z[:, 3 * W_A:3 * W_A + LORA_W + LORA_A]
    gd = z[:, 3 * W_A + LORA_W + LORA_A:]
    lane = _iota2(lo.shape, 1)
    lo = jnp.where(lane < LORA_W, jnp.tanh(lo), lo)
    up = _bdot(lo, wlora_ref[...])
    w = -_softplus(-(w0_ref[...] + up[:, :W_A])) - 0.5
    a = _sigmoid(a0_ref[...] + up[:, W_A:])
    g = _bdot(_sigmoid(gd), gup_ref[...])
    kk = k * kk_ref[...]
    ss = _ones_dot_right(kk * kk, _head_ones())
    kk = kk / jnp.maximum(jnp.sqrt(ss), 1e-12)
    r_out[...] = r
    lw_out[...] = -jnp.exp(w)
    k_out[...] = k * (1.0 + (a - 1.0) * ka_ref[...])
    v_out[...] = v
    kkn_out[...] = kk
    a_out[...] = a
    g_out[...] = g


def rwkv_pre(pa, pa_prev, seq_len, p):
    m = pa.shape[0]
    has_prev = pa_prev is not None
    tm = m if has_prev else min(512, seq_len)
    vec = lambda a: a.reshape(1, -1)
    ins = [pa] + ([pa_prev] if has_prev else []) + [
        vec(p['mu']), p['w_lora'], p['g_up'], vec(p['w0']), vec(p['a0']), vec(p['k_k']), vec(p['k_a'])]
    row = lambda n: pl.BlockSpec((tm, n), lambda i: (i, 0))
    full = lambda a: pl.BlockSpec(a.shape, lambda i: (0, 0))
    in_specs = [row(A_COLS)] + ([row(A_COLS)] if has_prev else []) + [full(a) for a in ins[-7:]]
    scratch = [] if has_prev else [pltpu.VMEM((1, A_COLS), F32)]
    return pl.pallas_call(
        functools.partial(_rwkv_pre_kernel, seq_blocks=max(seq_len // tm, 1), has_prev=has_prev),
        grid=(m // tm,), in_specs=in_specs, out_specs=[row(W_A)] * 7,
        out_shape=[jax.ShapeDtypeStruct((m, W_A), F32)] * 7, scratch_shapes=scratch,
        compiler_params=_cparams("arbitrary"), name="rwkv_pre")(*ins)


def _group_norm_gate(y, r, k, v, g, gng, gnb, rk):
    outs = []
    for h in range(N_HEADS):
        sl = slice(h * DH, (h + 1) * DH)
        yh = y[:, sl]
        mean = jnp.mean(yh, axis=-1, keepdims=True)
        var = jnp.mean(jnp.square(yh - mean), axis=-1, keepdims=True)
        yn = (yh - mean) * lax.rsqrt(var + GN_EPS) * gng[:, sl] + gnb[:, sl]
        bonus = jnp.sum(r[:, sl] * k[:, sl] * rk[:, sl], axis=-1, keepdims=True) * v[:, sl]
        outs.append(yn + bonus)
    return jnp.concatenate(outs, axis=-1) * g


def _rwkv_chunk_kernel(r_ref, lw_ref, k_ref, v_ref, kk_ref, a_ref, g_ref, gng_ref, gnb_ref, rk_ref,
                       y_ref, s_ref, s_scr):
    c = pl.program_id(1)
    C = RWKV_CHUNK

    @pl.when(c == 0)
    def _():
        s_scr[...] = jnp.zeros_like(s_scr)

    row = _iota2((C, C), 0)
    col = _iota2((C, C), 1)
    lw = lw_ref[...]
    cum = _ones_dot_left((col <= row).astype(F32), lw)
    e_in = jnp.exp(cum)
    e_ex = jnp.exp(cum - lw)
    e_neg = jnp.exp(-cum)
    r = r_ref[...]
    k = k_ref[...]
    v = v_ref[...]
    kk = kk_ref[...]
    a_hat = -kk * e_ex
    b_hat = kk * a_ref[...] * e_neg
    k_hat = k * e_neg
    r_hat = r * e_in

    row2 = _iota2((C, 2 * C), 0)
    col2 = _iota2((C, 2 * C), 1)
    strict = col < row
    second_strict = (col2 >= C) & (col2 - C < row2)
    lower2 = jnp.where(col2 >= C, col2 - C, col2) <= row2
    eye = (row == col).astype(F32)

    ys = []
    for h in range(N_HEADS):
        sl = slice(h * DH, (h + 1) * DH)
        s0 = s_scr[h]
        ar = jnp.concatenate([a_hat[:, sl], r_hat[:, sl]], axis=0)
        bk = jnp.concatenate([b_hat[:, sl], k_hat[:, sl]], axis=0)
        vh = v[:, sl]
        gmat = _bdot_nt(ar, bk)
        x = _bdot_nt(ar, s0)
        a_ab = jnp.where(strict, gmat[:C, :C], 0.0)
        g_ak = jnp.where(second_strict, gmat[:C, :], 0.0)
        g_r = jnp.where(lower2, gmat[C:, :], 0.0)
        vv = jnp.concatenate([vh, vh], axis=0)
        rhs = x[:C] + _bdot(g_ak, vv)
        t = eye + a_ab
        pw = a_ab
        for _ in range(int(math.log2(C)) - 1):
            pw = _bdot(pw, pw)
            t = t + _bdot(t, pw)
        u = _bdot(t, rhs)
        uv = jnp.concatenate([u, vh], axis=0)
        ys.append(x[C:] + _bdot(g_r, uv))
        s_scr[h] = (s0 + _bdot_tn(uv, bk)) * e_in[C - 1:C, sl]

    y = jnp.concatenate(ys, axis=-1)
    y_ref[...] = _group_norm_gate(y, r, k, v, g_ref[...], gng_ref[...], gnb_ref[...], rk_ref[...])

    @pl.when(c == pl.num_programs(1) - 1)
    def _():
        s_ref[...] = s_scr[...]


def rwkv_chunked(pre, batch, seq_len, p):
    C = RWKV_CHUNK
    nc = seq_len // C
    seq = lambda a: a.reshape(batch, seq_len, W_A)
    blk = pl.BlockSpec((None, C, W_A), lambda b, c: (b, c, 0))
    vec = lambda a: a.reshape(1, W_A)
    full = pl.BlockSpec((1, W_A), lambda b, c: (0, 0))
    y, s = pl.pallas_call(
        _rwkv_chunk_kernel, grid=(batch, nc),
        in_specs=[blk] * 7 + [full] * 3,
        out_specs=[blk, pl.BlockSpec((None, N_HEADS, DH, DH), lambda b, c: (b, 0, 0, 0))],
        out_shape=[jax.ShapeDtypeStruct((batch, seq_len, W_A), F32),
                   jax.ShapeDtypeStruct((batch, N_HEADS, DH, DH), F32)],
        scratch_shapes=[pltpu.VMEM((N_HEADS, DH, DH), F32)],
        compiler_params=_cparams("parallel", "arbitrary"), name="rwkv_chunk",
    )(*[seq(a) for a in pre], vec(p['gn_g']), vec(p['gn_b']), vec(p['r_k']))
    return y.reshape(batch * seq_len, W_A), s


def _rwkv_step_kernel(r_ref, lw_ref, k_ref, v_ref, kk_ref, a_ref, g_ref, gng_ref, gnb_ref, rk_ref,
                      s_ref, y_ref, so_ref):
    r = r_ref[...]
    k = k_ref[...]
    v = v_ref[...]
    kk = kk_ref[...]
    w = jnp.exp(lw_ref[...])
    b = kk * a_ref[...]
    eye = _iota2((DH, DH), 0) == _iota2((DH, DH), 1)
    ys = []
    for h in range(N_HEADS):
        sl = slice(h * DH, (h + 1) * DH)
        s = s_ref[h]
        sa = jnp.sum(s * (-kk[:, sl]), axis=-1, keepdims=True)
        v_col = jnp.sum(jnp.where(eye, v[:, sl], 0.0), axis=-1, keepdims=True)
        s = s * w[:, sl] + sa * b[:, sl] + v_col * k[:, sl]
        so_ref[h] = s
        y_col = jnp.sum(s * r[:, sl], axis=-1, keepdims=True)
        ys.append(jnp.sum(jnp.where(eye, y_col, 0.0), axis=0, keepdims=True))
    y = jnp.concatenate(ys, axis=-1)
    y_ref[...] = _group_norm_gate(y, r, k, v, g_ref[...], gng_ref[...], gnb_ref[...], rk_ref[...])


def rwkv_step(pre, state, p):
    batch = state.shape[0]
    tok = lambda a: a.reshape(batch, 1, W_A)
    blk = pl.BlockSpec((None, 1, W_A), lambda b: (b, 0, 0))
    sblk = pl.BlockSpec((None, N_HEADS, DH, DH), lambda b: (b, 0, 0, 0))
    vec = lambda a: a.reshape(1, W_A)
    full = pl.BlockSpec((1, W_A), lambda b: (0, 0))
    y, s = pl.pallas_call(
        _rwkv_step_kernel, grid=(batch,),
        in_specs=[blk] * 7 + [full] * 3 + [sblk], out_specs=[blk, sblk],
        out_shape=[jax.ShapeDtypeStruct((batch, 1, W_A), F32),
                   jax.ShapeDtypeStruct(state.shape, F32)],
        compiler_params=_cparams("parallel"), name="rwkv_step",
    )(*[tok(a) for a in pre], vec(p['gn_g']), vec(p['gn_b']), vec(p['r_k']), state)
    return y.reshape(batch, W_A), s


def _fox_gate_kernel(f_ref, bf_ref, lf_ref, c_ref, ct_ref):
    t = f_ref.shape[0]
    lf = _log_sigmoid(f_ref[...] + bf_ref[...])
    lf_ref[...] = lf[:, :N_HEADS]
    blk = LANE
    row = _iota2((blk, blk), 0)
    col = _iota2((blk, blk), 1)
    tri = (col <= row).astype(F32)
    carry = jnp.zeros((1, LANE), F32)
    for i in range(t // blk):
        c_blk = _ones_dot_left(tri, lf[i * blk:(i + 1) * blk]) + carry
        c_ref[i * blk:(i + 1) * blk, :] = c_blk
        ct_ref[i] = c_blk.T[:N_HEADS]
        carry = c_blk[blk - 1:blk, :]


def fox_gates(f_pad, b_f, batch, seq_len):
    bf = jnp.zeros((1, LANE), F32).at[0, :N_HEADS].set(b_f)
    return pl.pallas_call(
        _fox_gate_kernel, grid=(batch,),
        in_specs=[pl.BlockSpec((None, seq_len, LANE), lambda b: (b, 0, 0)),
                  pl.BlockSpec((1, LANE), lambda b: (0, 0))],
        out_specs=[pl.BlockSpec((None, seq_len, N_HEADS), lambda b: (b, 0, 0)),
                   pl.BlockSpec((None, seq_len, LANE), lambda b: (b, 0, 0)),
                   pl.BlockSpec((None, seq_len // LANE, N_HEADS, LANE), lambda b: (b, 0, 0, 0))],
        out_shape=[jax.ShapeDtypeStruct((batch, seq_len, N_HEADS), F32),
                   jax.ShapeDtypeStruct((batch, seq_len, LANE), F32),
                   jax.ShapeDtypeStruct((batch, seq_len // LANE, N_HEADS, LANE), F32)],
        compiler_params=_cparams("parallel"), name="fox_gates",
    )(f_pad.reshape(batch, seq_len, LANE), bf)


def _fox_prompt_kernel(q_ref, k_ref, v_ref, c_ref, ct_ref, o_ref, kb_scr, vlo_scr, vhi_scr, *, pair_axis):
    pr = pl.program_id(pair_axis)
    qi = pl.program_id(2)
    tq = q_ref.shape[0]
    tk = tq
    lane = _iota2((1, LANE), 1)
    lo = lane < DH

    @pl.when(qi == 0)
    def _():
        kb_scr[...] = k_ref[...].astype(BF16)
        v = v_ref[...]
        vlo_scr[...] = jnp.where(lo, v, 0.0).astype(BF16)
        vhi_scr[...] = jnp.where(lo, 0.0, v).astype(BF16)

    q = q_ref[...] * QK_SCALE
    q_lo = jnp.where(lo, q, 0.0).astype(BF16)
    q_hi = jnp.where(lo, 0.0, q).astype(BF16)
    c_all = c_ref[...]
    head_lane = _iota2(c_all.shape, 1)
    cq = [jnp.sum(jnp.where(head_lane == 2 * pr + j, c_all, 0.0), axis=-1, keepdims=True)
          for j in range(2)]
    qs = (q_lo, q_hi)
    vs = (vlo_scr, vhi_scr)

    def step(j, carry, masked):
        m, l, acc = carry
        off = pl.multiple_of(j * tk, tk)
        kb = kb_scr[pl.ds(off, tk), :]
        new_m, new_l, alphas, pvs = [], [], [], []
        for hh in range(2):
            ck = jnp.concatenate([ct_ref[j * (tk // LANE) + i, pl.ds(2 * pr + hh, 1), :]
                                  for i in range(tk // LANE)], axis=-1)
            s = _bdot_nt(qs[hh], kb) + cq[hh] - ck
            if masked:
                s = jnp.where(_iota2(s.shape, 1) <= _iota2(s.shape, 0), s, NEG_INF)
            mn = jnp.maximum(m[hh], jnp.max(s, axis=-1, keepdims=True))
            alpha = jnp.exp(m[hh] - mn)
            pmat = jnp.exp(s - mn)
            new_m.append(mn)
            new_l.append(l[hh] * alpha + jnp.sum(pmat, axis=-1, keepdims=True))
            alphas.append(alpha)
            pvs.append(jnp.dot(pmat.astype(BF16), vs[hh][pl.ds(off, tk), :], preferred_element_type=F32))
        acc = acc * jnp.where(lo, alphas[0], alphas[1]) + pvs[0] + pvs[1]
        return tuple(new_m), tuple(new_l), acc

    init = ((jnp.full((tq, 1), NEG_INF, F32),) * 2, (jnp.zeros((tq, 1), F32),) * 2,
            jnp.zeros((tq, LANE), F32))
    carry = lax.fori_loop(0, qi, lambda j, cr: step(j, cr, False), init)
    m, l, acc = step(qi, carry, True)
    o_ref[...] = acc / jnp.where(lo, l[0], l[1])


def fox_prompt(q, k, v, c, ct, batch, seq_len):
    tq = min(ATT_BLOCK, seq_len)
    seq = lambda a: a.reshape(batch, seq_len, W_A)
    kv_spec = pl.BlockSpec((None, seq_len, LANE), lambda b, pr, i: (b, 0, pr))
    o = pl.pallas_call(
        functools.partial(_fox_prompt_kernel, pair_axis=1),
        grid=(batch, N_HEADS // 2, seq_len // tq),
        in_specs=[pl.BlockSpec((None, tq, LANE), lambda b, pr, i: (b, i, pr)), kv_spec, kv_spec,
                  pl.BlockSpec((None, tq, LANE), lambda b, pr, i: (b, i, 0)),
                  pl.BlockSpec((None, seq_len // LANE, N_HEADS, LANE), lambda b, pr, i: (b, 0, 0, 0))],
        out_specs=pl.BlockSpec((None, tq, LANE), lambda b, pr, i: (b, i, pr)),
        out_shape=jax.ShapeDtypeStruct((batch, seq_len, W_A), F32),
        scratch_shapes=[pltpu.VMEM((seq_len, LANE), BF16)] * 3,
        compiler_params=_cparams("parallel", "parallel", "arbitrary"), name="fox_prompt",
    )(seq(q), seq(k), seq(v), c, ct)
    return o.reshape(batch * seq_len, W_A)


FOX_PAGES_PER_STEP = 8


def _fox_decode_kernel(pt_ref, q_ref, kn_ref, vn_ref, lfn_ref, *refs, n_pages):
    g = n_pages
    lf_refs = refs[0:g]
    k_refs = refs[g:2 * g]
    v_refs = refs[2 * g:3 * g]
    o_ref = refs[3 * g]
    qb_scr, m_scr, l_scr, acc_scr, tail_scr = refs[3 * g + 1:]
    step = pl.program_id(1)
    page = k_refs[0].shape[-1]
    eye = _iota2((DH, DH), 0) == _iota2((DH, DH), 1)

    @pl.when(step == 0)
    def _():
        m_scr[...] = jnp.full_like(m_scr, NEG_INF)
        l_scr[...] = jnp.zeros_like(l_scr)
        acc_scr[...] = jnp.zeros_like(acc_scr)
        tail_scr[...] = jnp.zeros_like(tail_scr)
        q = q_ref[...] * QK_SCALE
        for h in range(N_HEADS):
            q_col = jnp.sum(jnp.where(eye, q[:, h * DH:(h + 1) * DH], 0.0), axis=-1, keepdims=True)
            qb_scr[h] = jnp.broadcast_to(q_col, (DH, page))

    lfn = lfn_ref[...]
    after = (_iota2((page, page), 0) > _iota2((page, page), 1)).astype(F32)
    tail = tail_scr[...]
    decs = []
    for i in range(g):
        lf = lf_refs[i][...]
        decs.append(_ones_dot_right(lf, after) + tail + lfn)
        tail = tail + jnp.sum(lf, axis=-1, keepdims=True)
    tail_scr[...] = tail
    dec = jnp.concatenate(decs, axis=-1)
    for h in range(N_HEADS):
        qb = qb_scr[h]
        s = jnp.concatenate([jnp.sum(qb * k_refs[i][h], axis=0, keepdims=True) for i in range(g)],
                            axis=-1) + dec[h:h + 1, :]
        m_old = m_scr[h][:, :1]
        mn = jnp.maximum(m_old, jnp.max(s, axis=-1, keepdims=True))
        alpha = jnp.exp(m_old - mn)
        pmat = jnp.exp(s - mn)
        l_new = l_scr[h][:, :1] * alpha + jnp.sum(pmat, axis=-1, keepdims=True)
        m_scr[h] = jnp.broadcast_to(mn, (1, LANE))
        l_scr[h] = jnp.broadcast_to(l_new, (1, LANE))
        acc = acc_scr[h] * alpha
        for i in range(g):
            acc = acc + pmat[:, i * page:(i + 1) * page] * v_refs[i][h]
        acc_scr[h] = acc

    @pl.when(step == pl.num_programs(1) - 1)
    def _():
        q = q_ref[...] * QK_SCALE
        kn = kn_ref[...]
        vn = vn_ref[...]
        outs = []
        for h in range(N_HEADS):
            sl = slice(h * DH, (h + 1) * DH)
            s_new = jnp.sum(q[:, sl] * kn[:, sl], axis=-1, keepdims=True)
            m_old = m_scr[h][:, :1]
            mn = jnp.maximum(m_old, s_new)
            alpha = jnp.exp(m_old - mn)
            p_new = jnp.exp(s_new - mn)
            lt = l_scr[h][:, :1] * alpha + p_new
            o_col = jnp.sum(acc_scr[h], axis=-1, keepdims=True) * alpha
            o_row = jnp.sum(jnp.where(eye, o_col, 0.0), axis=0, keepdims=True)
            outs.append((o_row + p_new * vn[:, sl]) / lt)
        o_ref[...] = jnp.concatenate(outs, axis=-1)


def fox_decode(q, k_new, v_new, logf_new, cache_kt, cache_vt, cache_lf_t, layer, page_table):
    batch, n_pages = page_table.shape
    page = cache_kt.shape[-1]
    g = min(FOX_PAGES_PER_STEP, n_pages)
    steps = n_pages // g
    tok = lambda a: a.reshape(batch, 1, W_A)
    tok_spec = pl.BlockSpec((None, 1, W_A), lambda b, s, pt: (b, 0, 0))

    def page_idx(b, s, pt, i):
        return pt[b * n_pages + (n_pages - 1 - (s * g + i))]

    lf_specs = [pl.BlockSpec((None, None, N_HEADS, page),
                             lambda b, s, pt, i=i: (layer, page_idx(b, s, pt, i), 0, 0)) for i in range(g)]
    kv_specs = [pl.BlockSpec((None, None, N_HEADS, DH, page),
                             lambda b, s, pt, i=i: (layer, page_idx(b, s, pt, i), 0, 0, 0)) for i in range(g)]
    grid_spec = pltpu.PrefetchScalarGridSpec(
        num_scalar_prefetch=1, grid=(batch, steps),
        in_specs=[tok_spec, tok_spec, tok_spec,
                  pl.BlockSpec((None, N_HEADS, 1), lambda b, s, pt: (b, 0, 0))] + lf_specs + kv_specs + kv_specs,
        out_specs=tok_spec,
        scratch_shapes=[pltpu.VMEM((N_HEADS, DH, page), F32), pltpu.VMEM((N_HEADS, 1, LANE), F32),
                        pltpu.VMEM((N_HEADS, 1, LANE), F32), pltpu.VMEM((N_HEADS, DH, page), F32),
                        pltpu.VMEM((N_HEADS, 1), F32)])
    o = pl.pallas_call(
        functools.partial(_fox_decode_kernel, n_pages=g), grid_spec=grid_spec,
        out_shape=jax.ShapeDtypeStruct((batch, 1, W_A), F32),
        compiler_params=_cparams("parallel", "arbitrary"), name="fox_decode",
    )(page_table.reshape(-1), tok(q), tok(k_new), tok(v_new), logf_new.reshape(batch, N_HEADS, 1),
      *([cache_lf_t] * g), *([cache_kt] * g), *([cache_vt] * g))
    return o.reshape(batch, W_A)


def _t5_buckets(dist):
    n = np.maximum(dist, 0)
    max_exact = N_BUCKETS // 2
    nf = np.maximum(n, 1).astype(np.float32)
    large = max_exact + (np.log(nf / max_exact) / math.log(MAX_DISTANCE / max_exact)
                         * (N_BUCKETS - max_exact)).astype(np.int32)
    large = np.minimum(large, N_BUCKETS - 1)
    return np.where(n < max_exact, n, large).astype(np.int32)


def _bias_gather_kernel(idx_ref, table_ref, o_ref):
    col = pl.program_id(0)
    idx = idx_ref[...]
    out = jnp.zeros(idx.shape, F32)
    for bkt in range(N_BUCKETS):
        out = jnp.where(idx == bkt, table_ref[bkt, col], out)
    o_ref[...] = out


def bias_gather(idx, table):
    r, c = idx.shape
    ncol = table.shape[1]
    return pl.pallas_call(
        _bias_gather_kernel, grid=(ncol,),
        in_specs=[pl.BlockSpec((r, c), lambda j: (0, 0)),
                  pl.BlockSpec(memory_space=pltpu.SMEM)],
        out_specs=pl.BlockSpec((None, r, c), lambda j: (j, 0, 0)),
        out_shape=jax.ShapeDtypeStruct((ncol, r, c), F32),
        compiler_params=_cparams("parallel"), name="bias_gather")(idx, table)


def _diff_lambda(lq1, lk1, lq2, lk2, lambda_init):
    return (jnp.exp(jnp.sum(lq1 * lk1, axis=-1, keepdims=True))
            - jnp.exp(jnp.sum(lq2 * lk2, axis=-1, keepdims=True)) + lambda_init)


def _diff_prompt_kernel(q_ref, k_ref, v_ref, bias_ref, far_ref, lam_ref, sub_ref, o_ref, kb_scr, vb_scr,
                        *, lambda_init):
    qi = pl.program_id(2)
    tq = q_ref.shape[0]
    tk = tq
    lane = _iota2((1, LANE), 1)
    lo = lane < DH

    @pl.when(qi == 0)
    def _():
        kb_scr[...] = k_ref[...].astype(BF16)
        vb_scr[...] = v_ref[...].astype(BF16)

    q = q_ref[...] * QK_SCALE
    qs = (jnp.where(lo, q, 0.0).astype(BF16), jnp.where(lo, 0.0, q).astype(BF16))
    head = pl.program_id(1)
    fars = (far_ref[head, 0], far_ref[head, 1])

    def step(j, carry, kind):
        m, l, acc = carry
        off = pl.multiple_of(j * tk, tk)
        kb = kb_scr[pl.ds(off, tk), :]
        vb = vb_scr[pl.ds(off, tk), :]
        new_m, new_l, new_acc = [], [], []
        for mp in range(2):
            s = _bdot_nt(qs[mp], kb)
            if kind == 'far':
                s = s + fars[mp]
            elif kind == 'near':
                s = s + bias_ref[mp, 1]
            else:
                s = s + bias_ref[mp, 0]
                s = jnp.where(_iota2(s.shape, 1) <= _iota2(s.shape, 0), s, NEG_INF)
            mn = jnp.maximum(m[mp], jnp.max(s, axis=-1, keepdims=True))
            alpha = jnp.exp(m[mp] - mn)
            pmat = jnp.exp(s - mn)
            new_m.append(mn)
            new_l.append(l[mp] * alpha + jnp.sum(pmat, axis=-1, keepdims=True))
            new_acc.append(acc[mp] * alpha + jnp.dot(pmat.astype(BF16), vb, preferred_element_type=F32))
        return tuple(new_m), tuple(new_l), tuple(new_acc)

    init = ((jnp.full((tq, 1), NEG_INF, F32),) * 2, (jnp.zeros((tq, 1), F32),) * 2,
            (jnp.zeros((tq, LANE), F32),) * 2)
    carry = lax.fori_loop(0, jnp.maximum(qi - 1, 0), lambda j, cr: step(j, cr, 'far'), init)
    carry = lax.cond(qi >= 1, lambda cr: step(qi - 1, cr, 'near'), lambda cr: cr, carry)
    m, l, acc = step(qi, carry, 'diag')
    lam = _diff_lambda(lam_ref[0:1, :], lam_ref[1:2, :], lam_ref[2:3, :], lam_ref[3:4, :], lambda_init)
    o = acc[0] / l[0] - lam * (acc[1] / l[1])
    o_ref[...] = _rmsnorm(o, sub_ref[...]) * (1.0 - lambda_init)


def diff_prompt(q, k, v, bias_tiles, far_bias, lam_vecs, subln_g, lambda_init, batch, seq_len):
    tq = min(ATT_BLOCK, seq_len)
    seq = lambda a: a.reshape(batch, seq_len, W_C)
    kv_spec = pl.BlockSpec((None, seq_len, LANE), lambda b, h, i: (b, 0, h))
    blk = pl.BlockSpec((None, tq, LANE), lambda b, h, i: (b, i, h))
    o = pl.pallas_call(
        functools.partial(_diff_prompt_kernel, lambda_init=lambda_init),
        grid=(batch, N_HEADS, seq_len // tq),
        in_specs=[blk, kv_spec, kv_spec,
                  pl.BlockSpec((None, 2, 2, tq, tq), lambda b, h, i: (h, 0, 0, 0, 0)),
                  pl.BlockSpec(memory_space=pltpu.SMEM),
                  pl.BlockSpec((4, DH), lambda b, h, i: (0, 0)),
                  pl.BlockSpec((1, LANE), lambda b, h, i: (0, 0))],
        out_specs=blk,
        out_shape=jax.ShapeDtypeStruct((batch, seq_len, W_C), F32),
        scratch_shapes=[pltpu.VMEM((seq_len, LANE), BF16)] * 2,
        compiler_params=_cparams("parallel", "parallel", "arbitrary"), name="diff_prompt",
    )(seq(q), seq(k), seq(v), bias_tiles, far_bias, lam_vecs, subln_g.reshape(1, LANE))
    return o.reshape(batch * seq_len, W_C)


DIFF_PAGES_PER_STEP = 4


def _diff_decode_kernel(pt_ref, q_ref, kn_ref, vn_ref, bias_ref, rel_ref, lam_ref, sub_ref, *refs,
                        n_pages, lambda_init):
    g = n_pages
    k_refs = refs[0:g]
    v_refs = refs[g:2 * g]
    o_ref = refs[2 * g]
    m_scr, l_scr, acc_scr = refs[2 * g + 1:]
    step = pl.program_id(1)

    @pl.when(step == 0)
    def _():
        m_scr[...] = jnp.full_like(m_scr, NEG_INF)
        l_scr[...] = jnp.zeros_like(l_scr)
        acc_scr[...] = jnp.zeros_like(acc_scr)

    row = _iota2((N_HEADS, LANE), 0)
    lane = _iota2((N_HEADS, LANE), 1)
    qsel = ((row == 0) & (lane < DH)) | ((row == 1) & (lane >= DH))

    def q_tile(h):
        return jnp.where(qsel, q_ref[h:h + 1, :] * QK_SCALE, 0.0)

    for h in range(N_HEADS):
        kh = jnp.concatenate([k_refs[i][:, h, :] for i in range(g)], axis=0)
        vh = jnp.concatenate([v_refs[i][:, h, :] for i in range(g)], axis=0)
        s = lax.dot_general(q_tile(h), kh, (((1,), (1,)), ((), ())),
                            preferred_element_type=F32) + bias_ref[h]
        m_old = m_scr[h][:, :1]
        mn = jnp.maximum(m_old, jnp.max(s, axis=-1, keepdims=True))
        alpha = jnp.exp(m_old - mn)
        pmat = jnp.exp(s - mn)
        l_new = l_scr[h][:, :1] * alpha + jnp.sum(pmat, axis=-1, keepdims=True)
        m_scr[h] = jnp.broadcast_to(mn, (N_HEADS, LANE))
        l_scr[h] = jnp.broadcast_to(l_new, (N_HEADS, LANE))
        acc_scr[h] = acc_scr[h] * alpha + jnp.dot(pmat, vh, preferred_element_type=F32)

    @pl.when(step == pl.num_programs(1) - 1)
    def _():
        lam = _diff_lambda(lam_ref[0:1, :], lam_ref[1:2, :], lam_ref[2:3, :], lam_ref[3:4, :], lambda_init)
        for h in range(N_HEADS):
            b_new = jnp.where(row[:, :1] == 0, rel_ref[0, 2 * h], rel_ref[0, 2 * h + 1])
            s_new = jnp.sum(q_tile(h) * kn_ref[h:h + 1, :], axis=-1, keepdims=True) + b_new
            m_old = m_scr[h][:, :1]
            mn = jnp.maximum(m_old, s_new)
            alpha = jnp.exp(m_old - mn)
            p_new = jnp.exp(s_new - mn)
            lt = l_scr[h][:, :1] * alpha + p_new
            o = (acc_scr[h] * alpha + p_new * vn_ref[h:h + 1, :]) / lt
            od = o[0:1] - lam * o[1:2]
            od = od * lax.rsqrt(jnp.mean(od * od, axis=-1, keepdims=True) + RMS_EPS)
            o_ref[:, h * LANE:(h + 1) * LANE] = od * sub_ref[...] * (1.0 - lambda_init)


def diff_decode(q, k_new, v_new, cache_k, cache_v, layer, page_table, page_bias, rel_bias, lam_vecs,
                subln_g, lambda_init):
    batch, n_pages = page_table.shape
    page = cache_k.shape[2]
    g = min(DIFF_PAGES_PER_STEP, n_pages)
    steps = n_pages // g
    heads = lambda a: a.reshape(batch, N_HEADS, LANE)
    head_spec = pl.BlockSpec((None, N_HEADS, LANE), lambda b, s, pt: (b, 0, 0))
    kv_specs = [pl.BlockSpec((None, None, page, N_HEADS, LANE),
                             lambda b, s, pt, i=i: (layer, pt[b * n_pages + s * g + i], 0, 0, 0))
                for i in range(g)]
    grid_spec = pltpu.PrefetchScalarGridSpec(
        num_scalar_prefetch=1, grid=(batch, steps),
        in_specs=[head_spec, head_spec, head_spec,
                  pl.BlockSpec((None, N_HEADS, N_HEADS, g * page), lambda b, s, pt: (s, 0, 0, 0)),
                  pl.BlockSpec(memory_space=pltpu.SMEM),
                  pl.BlockSpec((4, DH), lambda b, s, pt: (0, 0)),
                  pl.BlockSpec((1, LANE), lambda b, s, pt: (0, 0))] + kv_specs + kv_specs,
        out_specs=pl.BlockSpec((None, 1, W_C), lambda b, s, pt: (b, 0, 0)),
        scratch_shapes=[pltpu.VMEM((N_HEADS, N_HEADS, LANE), F32)] * 3)
    o = pl.pallas_call(
        functools.partial(_diff_decode_kernel, n_pages=g, lambda_init=lambda_init), grid_spec=grid_spec,
        out_shape=jax.ShapeDtypeStruct((batch, 1, W_C), F32),
        compiler_params=_cparams("parallel", "arbitrary"), name="diff_decode",
    )(page_table.reshape(-1), heads(q), heads(k_new), heads(v_new), page_bias, rel_bias, lam_vecs,
      subln_g.reshape(1, LANE), *([cache_k] * g), *([cache_v] * g))
    return o.reshape(batch, W_C)


def _prep_params(norm_mix, norm_ffn, norm_final, w_in_ab, w_out_ab, rwkv_mu, rwkv_w0, rwkv_w_up, rwkv_a0,
                 rwkv_a_up, rwkv_g_up, rwkv_k_k, rwkv_k_a, rwkv_r_k, rwkv_gn_g, rwkv_gn_b, fox_b_f, w_in_c,
                 w_out_c, diff_lq1, diff_lk1, diff_lq2, diff_lk2, diff_subln, rel_bias, ffn_gate, ffn_up,
                 ffn_down):
    depth = norm_mix.shape[0]
    o_r, o_wd, o_k, o_v, o_ad, o_gd = np.cumsum([0, W_A, LORA_W, W_A, W_A, LORA_A]).tolist()
    perm = np.concatenate([np.arange(o_r, o_r + W_A), np.arange(o_k, o_k + W_A), np.arange(o_v, o_v + W_A),
                           np.arange(o_wd, o_wd + LORA_W), np.arange(o_ad, o_ad + LORA_A),
                           np.arange(o_gd, o_gd + LORA_G)])
    layers = []
    for li in range(depth):
        j = li // 2
        lp = {'norm_mix': norm_mix[li], 'norm_ffn': norm_ffn[li],
              'ffn_gate': ffn_gate[li].astype(BF16), 'ffn_up': ffn_up[li].astype(BF16),
              'ffn_down': ffn_down[li].astype(BF16)}
        if li % 2 == 0:
            w = w_in_ab[j]
            wa = w[:, :A_COLS][:, perm]
            wb = w[:, A_COLS:]
            wf = jnp.pad(wb[:, 3 * W_A:], ((0, 0), (0, LANE - N_HEADS)))
            lp['w_in'] = [wa.astype(BF16), wb[:, :W_A].astype(BF16), wb[:, W_A:2 * W_A].astype(BF16),
                          wb[:, 2 * W_A:3 * W_A].astype(BF16), wf.astype(BF16)]
            lp['w_out'] = [w_out_ab[j][:W_A].astype(BF16), w_out_ab[j][W_A:].astype(BF16)]
            zeros = jnp.zeros((LORA_W, W_A), F32)
            w_lora = jnp.concatenate([jnp.concatenate([rwkv_w_up[j], zeros], axis=1),
                                      jnp.concatenate([zeros, rwkv_a_up[j]], axis=1)], axis=0)
            lp['rwkv'] = {'mu': rwkv_mu[j][perm], 'w_lora': w_lora.astype(BF16),
                          'g_up': rwkv_g_up[j].astype(BF16), 'w0': rwkv_w0[j], 'a0': rwkv_a0[j],
                          'k_k': rwkv_k_k[j], 'k_a': rwkv_k_a[j], 'r_k': rwkv_r_k[j].reshape(-1),
                          'gn_g': rwkv_gn_g[j], 'gn_b': rwkv_gn_b[j]}
            lp['fox_b_f'] = fox_b_f[j]
        else:
            w = w_in_c[j]
            lp['w_in'] = [w[:, :W_C].astype(BF16), w[:, W_C:2 * W_C].astype(BF16), w[:, 2 * W_C:].astype(BF16)]
            lp['w_out'] = [w_out_c[j].astype(BF16)]
            lp['lam_vecs'] = jnp.stack([diff_lq1[j], diff_lk1[j], diff_lq2[j], diff_lk2[j]])
            lp['subln'] = diff_subln[j]
            lp['lambda_init'] = 0.8 - 0.6 * math.exp(-0.3 * li)
        layers.append(lp)
    return layers, norm_final, rel_bias


def _prompt_bias(rel_bias, seq_len):
    tq = min(ATT_BLOCK, seq_len)
    d = np.arange(tq)[:, None] - np.arange(tq)[None, :]
    idx = np.concatenate([_t5_buckets(d), _t5_buckets(d + tq)], axis=0)
    tiles = bias_gather(jnp.asarray(idx), rel_bias)
    tiles = tiles.reshape(N_HEADS, 2, 2, tq, tq)
    far_idx = np.full((8, LANE), int(_t5_buckets(np.array([2 * tq]))[0]), np.int32)
    far = bias_gather(jnp.asarray(far_idx), rel_bias)[:, 0, 0].reshape(N_HEADS, 2)
    return tiles, far


def _decode_bias(rel_bias, n_pages, page, g):
    t0 = n_pages * page
    steps = n_pages // g
    rows = -(-steps // 8) * 8
    idx = np.zeros((rows, g * page), np.int32)
    idx[:steps] = _t5_buckets(t0 - np.arange(t0).reshape(steps, g * page))
    bias = bias_gather(jnp.asarray(idx), rel_bias)[:, :steps]
    bias = jnp.transpose(bias.reshape(N_HEADS, 2, steps, g * page), (2, 0, 1, 3))
    return jnp.pad(bias, ((0, 0), (0, 0), (0, N_HEADS - 2), (0, 0)))


def _decode_logf_kernel(f_ref, bf_ref, o_ref):
    o_ref[...] = _log_sigmoid(f_ref[...] + bf_ref[...])[:, :N_HEADS]


def _decode_logf(f_pad, b_f):
    bf = jnp.zeros((1, LANE), F32).at[0, :N_HEADS].set(b_f)
    return pl.pallas_call(
        _decode_logf_kernel, out_shape=jax.ShapeDtypeStruct((f_pad.shape[0], N_HEADS), F32),
        name="decode_logf")(f_pad, bf)


def _trunk(x, layers, norm_final, batch, seq_len, past, prompt_bias=None):
    fk, fv, fl, dk, dv, rs, sh = [], [], [], [], [], [], []
    y = None
    for li, lp in enumerate(layers):
        j = li // 2
        if li % 2 == 0:
            pa, q, k, v, f_pad, xn = norm_proj(x, lp['norm_mix'], lp['w_in'], emit_xn=True)
            if past is None:
                pre = rwkv_pre(pa, None, seq_len, lp['rwkv'])
                ya, s_fin = rwkv_chunked(pre, batch, seq_len, lp['rwkv'])
                logf, c, ct = fox_gates(f_pad, lp['fox_b_f'], batch, seq_len)
                yb = fox_prompt(q, k, v, c, ct, batch, seq_len)
            else:
                (pa_prev,) = norm_proj(past['state_shift'][j], None, lp['w_in'][:1])
                pre = rwkv_pre(pa, pa_prev, seq_len, lp['rwkv'])
                ya, s_fin = rwkv_step(pre, past['state_rwkv'][j], lp['rwkv'])
                logf = _decode_logf(f_pad, lp['fox_b_f'])
                yb = fox_decode(q, k, v, logf, past['cache_fox_kt'], past['cache_fox_vt'],
                                past['cache_fox_logf_t'], j, past['page_table'])
                logf = logf.reshape(batch, 1, N_HEADS)
            x = proj_residual(x, [ya, yb], lp['w_out'])
            fk.append(k.reshape(batch, seq_len, N_HEADS, DH))
            fv.append(v.reshape(batch, seq_len, N_HEADS, DH))
            fl.append(logf)
            rs.append(s_fin)
            sh.append(xn.reshape(batch, seq_len, D_MODEL)[:, -1])
        else:
            q, k, v = norm_proj(x, lp['norm_mix'], lp['w_in'])
            if past is None:
                yc = diff_prompt(q, k, v, prompt_bias[0], prompt_bias[1], lp['lam_vecs'],
                                 lp['subln'], lp['lambda_init'], batch, seq_len)
            else:
                yc = diff_decode(q, k, v, past['cache_diff_k'], past['cache_diff_v'], j, past['page_table'],
                                 past['page_bias'], past['rel_bias'], lp['lam_vecs'], lp['subln'],
                                 lp['lambda_init'])
            x = proj_residual(x, [yc], lp['w_out'])
            dk.append(k.reshape(batch, seq_len, N_HEADS, 2 * DH))
            dv.append(v.reshape(batch, seq_len, N_HEADS, 2 * DH))
        if li == len(layers) - 1:
            x, y = ffn_residual(x, lp['norm_ffn'], lp['ffn_gate'], lp['ffn_up'], lp['ffn_down'], norm_final)
        else:
            x = ffn_residual(x, lp['norm_ffn'], lp['ffn_gate'], lp['ffn_up'], lp['ffn_down'])
    return (y.reshape(batch, seq_len, D_MODEL), jnp.stack(fk), jnp.stack(fv), jnp.stack(fl), jnp.stack(dk),
            jnp.stack(dv), jnp.stack(rs), jnp.stack(sh))


def kernel(x_prompt, x_sample, cache_fox_k, cache_fox_v, cache_fox_logf, cache_diff_k, cache_diff_v, state_rwkv, state_shift, page_table, norm_mix, norm_ffn, norm_final, w_in_ab, w_out_ab, rwkv_mu, rwkv_w0, rwkv_w_up, rwkv_a0, rwkv_a_up, rwkv_g_up, rwkv_k_k, rwkv_k_a, rwkv_r_k, rwkv_gn_g, rwkv_gn_b, fox_b_f, w_in_c, w_out_c, diff_lq1, diff_lk1, diff_lq2, diff_lk2, diff_subln, rel_bias, ffn_gate, ffn_up, ffn_down):
    layers, norm_final, rel_bias = _prep_params(
        norm_mix, norm_ffn, norm_final, w_in_ab, w_out_ab, rwkv_mu, rwkv_w0, rwkv_w_up, rwkv_a0, rwkv_a_up,
        rwkv_g_up, rwkv_k_k, rwkv_k_a, rwkv_r_k, rwkv_gn_g, rwkv_gn_b, fox_b_f, w_in_c, w_out_c, diff_lq1,
        diff_lk1, diff_lq2, diff_lk2, diff_subln, rel_bias, ffn_gate, ffn_up, ffn_down)
    bp, tp, d = x_prompt.shape
    bs, ts, _ = x_sample.shape
    n_pages = page_table.shape[1]
    page = cache_fox_k.shape[2]
    past = {
        'state_shift': state_shift, 'state_rwkv': state_rwkv, 'page_table': page_table,
        'cache_fox_kt': jnp.transpose(cache_fox_k, (0, 1, 3, 4, 2)),
        'cache_fox_vt': jnp.transpose(cache_fox_v, (0, 1, 3, 4, 2)),
        'cache_fox_logf_t': jnp.swapaxes(cache_fox_logf, 2, 3),
        'cache_diff_k': cache_diff_k, 'cache_diff_v': cache_diff_v, 'rel_bias': rel_bias,
        'page_bias': _decode_bias(rel_bias, n_pages, page, min(DIFF_PAGES_PER_STEP, n_pages)),
    }
    out_p = _trunk(x_prompt.reshape(bp * tp, d), layers, norm_final, bp, tp, None,
                   _prompt_bias(rel_bias, tp))
    out_s = _trunk(x_sample.reshape(bs * ts, d), layers, norm_final, bs, ts, past)
    return (out_p[0], out_s[0]) + out_p[1:] + out_s[1:]
```

```python
import functools
import math

import jax
import jax.numpy as jnp
import numpy as np
from jax import lax
from jax.experimental import pallas as pl
from jax.experimental.pallas import tpu as pltpu

F32 = jnp.float32
BF16 = jnp.bfloat16

D_MODEL = 1024
N_HEADS = 8
DH = 64
W_A = N_HEADS * DH
LORA_W = 64
LORA_A = 64
LORA_G = 128
A_COLS = 3 * W_A + LORA_W + LORA_A + LORA_G
W_C = N_HEADS * 2 * DH
N_BUCKETS = 32
MAX_DISTANCE = 128
RMS_EPS = 1e-6
GN_EPS = 64e-5
NEG_INF = -1e30
QK_SCALE = DH ** -0.5

LANE = 128
RWKV_CHUNK = 64
RWKV_GROUP = 4
ATT_BLOCK = 512
VMEM_LIMIT = 48 * 1024 * 1024


def _cparams(*sem):
    return pltpu.CompilerParams(dimension_semantics=sem, vmem_limit_bytes=VMEM_LIMIT)


def _bdot(a, b):
    return jnp.dot(a.astype(BF16), b.astype(BF16), preferred_element_type=F32)


def _bdot_nt(a, b):
    return lax.dot_general(a.astype(BF16), b.astype(BF16), (((1,), (1,)), ((), ())),
                           preferred_element_type=F32)


def _bdot_tn(a, b):
    return lax.dot_general(a.astype(BF16), b.astype(BF16), (((0,), (0,)), ((), ())),
                           preferred_element_type=F32)


def _split3(x):
    hi = x.astype(BF16).astype(F32)
    r1 = x - hi
    mid = r1.astype(BF16).astype(F32)
    lo = (r1 - mid).astype(BF16).astype(F32)
    return hi, mid, lo


def _ones_dot_left(ones_mat, x):
    n = x.shape[1]
    stack = jnp.concatenate(_split3(x), axis=1).astype(BF16)
    out = jnp.dot(ones_mat.astype(BF16), stack, preferred_element_type=F32)
    return out[:, :n] + out[:, n:2 * n] + out[:, 2 * n:]


def _ones_dot_right(x, ones_mat):
    m = x.shape[0]
    stack = jnp.concatenate(_split3(x), axis=0).astype(BF16)
    out = jnp.dot(stack, ones_mat.astype(BF16), preferred_element_type=F32)
    return out[:m] + out[m:2 * m] + out[2 * m:]


def _sigmoid(x):
    return 1.0 / (1.0 + jnp.exp(-x))


def _log_sigmoid(x):
    return jnp.minimum(x, 0.0) - jnp.log(1.0 + jnp.exp(-jnp.abs(x)))


def _softplus(x):
    return jnp.maximum(x, 0.0) + jnp.log(1.0 + jnp.exp(-jnp.abs(x)))


def _rmsnorm(x, g):
    return x * lax.rsqrt(jnp.mean(x * x, axis=-1, keepdims=True) + RMS_EPS) * g


def _iota2(shape, dim):
    return lax.broadcasted_iota(jnp.int32, shape, dim)


def _row_tile(m):
    for t in (512, 256, 128):
        if m % t == 0:
            return t
    return m


def _norm_proj_kernel(*refs, n_out, with_norm, emit_xn):
    x_ref = refs[0]
    pos = 1
    if with_norm:
        g_ref = refs[pos]
        pos += 1
    w_refs = refs[pos:pos + n_out]
    pos += n_out
    out_refs = refs[pos:pos + n_out]
    pos += n_out
    x = x_ref[...]
    if with_norm:
        x = _rmsnorm(x, g_ref[...])
        if emit_xn:
            refs[pos][...] = x
    xb = x.astype(BF16)
    for w_ref, o_ref in zip(w_refs, out_refs):
        o_ref[...] = jnp.dot(xb, w_ref[...], preferred_element_type=F32)


def norm_proj(x, g, ws, emit_xn=False):
    m, d = x.shape
    tm = _row_tile(m)
    with_norm = g is not None
    ins = [x]
    in_specs = [pl.BlockSpec((tm, d), lambda i: (i, 0))]
    if with_norm:
        ins.append(g.reshape(1, d))
        in_specs.append(pl.BlockSpec((1, d), lambda i: (0, 0)))
    for w in ws:
        ins.append(w)
        in_specs.append(pl.BlockSpec(w.shape, lambda i: (0, 0)))
    out_shape = [jax.ShapeDtypeStruct((m, w.shape[1]), F32) for w in ws]
    out_specs = [pl.BlockSpec((tm, w.shape[1]), lambda i: (i, 0)) for w in ws]
    if emit_xn:
        out_shape.append(jax.ShapeDtypeStruct((m, d), F32))
        out_specs.append(pl.BlockSpec((tm, d), lambda i: (i, 0)))
    return pl.pallas_call(
        functools.partial(_norm_proj_kernel, n_out=len(ws), with_norm=with_norm, emit_xn=emit_xn),
        grid=(m // tm,), in_specs=in_specs, out_specs=out_specs, out_shape=out_shape,
        compiler_params=_cparams("parallel"), name="norm_proj")(*ins)


def _proj_residual_kernel(*refs, n_in):
    x_ref = refs[0]
    y_refs = refs[1:1 + n_in]
    w_refs = refs[1 + n_in:1 + 2 * n_in]
    o_ref = refs[1 + 2 * n_in]
    acc = x_ref[...]
    for y_ref, w_ref in zip(y_refs, w_refs):
        acc = acc + jnp.dot(y_ref[...].astype(BF16), w_ref[...], preferred_element_type=F32)
    o_ref[...] = acc


def proj_residual(x, ys, ws):
    m, d = x.shape
    tm = _row_tile(m)
    in_specs = [pl.BlockSpec((tm, d), lambda i: (i, 0))]
    in_specs += [pl.BlockSpec((tm, y.shape[1]), lambda i: (i, 0)) for y in ys]
    in_specs += [pl.BlockSpec(w.shape, lambda i: (0, 0)) for w in ws]
    return pl.pallas_call(
        functools.partial(_proj_residual_kernel, n_in=len(ys)),
        grid=(m // tm,), in_specs=in_specs, out_specs=pl.BlockSpec((tm, d), lambda i: (i, 0)),
        out_shape=jax.ShapeDtypeStruct((m, d), F32),
        compiler_params=_cparams("parallel"), name="proj_residual")(x, *ys, *ws)


FFN_TILE = 256


def _ffn_kernel(*refs, final_norm):
    if final_norm:
        x_ref, g_ref, wg_ref, wu_ref, wd_ref, gf_ref, o_ref, y_ref, xn_scr, acc_scr = refs
    else:
        x_ref, g_ref, wg_ref, wu_ref, wd_ref, o_ref, xn_scr, acc_scr = refs
    f = pl.program_id(1)

    @pl.when(f == 0)
    def _():
        xn_scr[...] = _rmsnorm(x_ref[...], g_ref[...]).astype(BF16)
        acc_scr[...] = jnp.zeros_like(acc_scr)

    xn = xn_scr[...]
    gate = jnp.dot(xn, wg_ref[...], preferred_element_type=F32)
    up = jnp.dot(xn, wu_ref[...], preferred_element_type=F32)
    h = (gate * _sigmoid(gate) * up).astype(BF16)
    acc_scr[...] += jnp.dot(h, wd_ref[...], preferred_element_type=F32)

    @pl.when(f == pl.num_programs(1) - 1)
    def _():
        out = x_ref[...] + acc_scr[...]
        o_ref[...] = out
        if final_norm:
            y_ref[...] = _rmsnorm(out, gf_ref[...])


def ffn_residual(x, g, wg, wu, wd, final_g=None):
    m, d = x.shape
    dff = wg.shape[1]
    tm = 1024 if m % 1024 == 0 else _row_tile(m)
    nf = dff // FFN_TILE
    final_norm = final_g is not None
    ins = [x, g.reshape(1, d), wg, wu, wd]
    in_specs = [pl.BlockSpec((tm, d), lambda i, f: (i, 0)),
                pl.BlockSpec((1, d), lambda i, f: (0, 0)),
                pl.BlockSpec((d, FFN_TILE), lambda i, f: (0, f)),
                pl.BlockSpec((d, FFN_TILE), lambda i, f: (0, f)),
                pl.BlockSpec((FFN_TILE, d), lambda i, f: (f, 0))]
    out_shape = [jax.ShapeDtypeStruct((m, d), F32)]
    out_specs = [pl.BlockSpec((tm, d), lambda i, f: (i, 0))]
    if final_norm:
        ins.append(final_g.reshape(1, d))
        in_specs.append(pl.BlockSpec((1, d), lambda i, f: (0, 0)))
        out_shape.append(jax.ShapeDtypeStruct((m, d), F32))
        out_specs.append(pl.BlockSpec((tm, d), lambda i, f: (i, 0)))
    res = pl.pallas_call(
        functools.partial(_ffn_kernel, final_norm=final_norm),
        grid=(m // tm, nf), in_specs=in_specs, out_specs=out_specs, out_shape=out_shape,
        scratch_shapes=[pltpu.VMEM((tm, d), BF16), pltpu.VMEM((tm, d), F32)],
        compiler_params=_cparams("parallel", "arbitrary"), name="ffn")(*ins)
    return res if final_norm else res[0]


def _head_ones():
    r = _iota2((W_A, W_A), 0) // DH
    c = _iota2((W_A, W_A), 1) // DH
    return (r == c).astype(F32)


def _rwkv_pre_kernel(*refs, seq_blocks, has_prev):
    if has_prev:
        (pa_ref, prev_ref, mu_ref, wlora_ref, gup_ref, w0_ref, a0_ref, kk_ref, ka_ref,
         r_out, lw_out, k_out, v_out, kkn_out, a_out, g_out) = refs
    else:
        (pa_ref, mu_ref, wlora_ref, gup_ref, w0_ref, a0_ref, kk_ref, ka_ref,
         r_out, lw_out, k_out, v_out, kkn_out, a_out, g_out, carry) = refs
    pa = pa_ref[...]
    tm = pa.shape[0]
    if has_prev:
        shifted = prev_ref[...]
    else:
        i = pl.program_id(0)
        first = (i % seq_blocks) == 0
        prev_row = jnp.where(first, 0.0, carry[...])
        rolled = pltpu.roll(pa, 1, 0)
        shifted = jnp.where(_iota2(pa.shape, 0) == 0, prev_row, rolled)
        carry[...] = pa[tm - 1:tm, :]
    z = pa + (shifted - pa) * mu_ref[...]
    r = z[:, 0:W_A]
    k = z[:, W_A:2 * W_A]
    v = z[:, 2 * W_A:3 * W_A]
    lo = z[:, 3 * W_A:3 * W_A + LORA_W + LORA_A]
    gd = z[:, 3 * W_A + LORA_W + LORA_A:]
    lane = _iota2(lo.shape, 1)
    lo = jnp.where(lane < LORA_W, jnp.tanh(lo), lo)
    up = _bdot(lo, wlora_ref[...])
    w = -_softplus(-(w0_ref[...] + up[:, :W_A])) - 0.5
    a = _sigmoid(a0_ref[...] + up[:, W_A:])
    g = _bdot(_sigmoid(gd), gup_ref[...])
    kk = k * kk_ref[...]
    ss = _ones_dot_right(kk * kk, _head_ones())
    kk = kk / jnp.maximum(jnp.sqrt(ss), 1e-12)
    r_out[...] = r
    lw_out[...] = -jnp.exp(w)
    k_out[...] = k * (1.0 + (a - 1.0) * ka_ref[...])
    v_out[...] = v
    kkn_out[...] = kk
    a_out[...] = a
    g_out[...] = g


def rwkv_pre(pa, pa_prev, seq_len, p):
    m = pa.shape[0]
    has_prev = pa_prev is not None
    tm = m if has_prev else min(512, seq_len)
    vec = lambda a: a.reshape(1, -1)
    ins = [pa] + ([pa_prev] if has_prev else []) + [
        vec(p['mu']), p['w_lora'], p['g_up'], vec(p['w0']), vec(p['a0']), vec(p['k_k']), vec(p['k_a'])]
    row = lambda n: pl.BlockSpec((tm, n), lambda i: (i, 0))
    full = lambda a: pl.BlockSpec(a.shape, lambda i: (0, 0))
    in_specs = [row(A_COLS)] + ([row(A_COLS)] if has_prev else []) + [full(a) for a in ins[-7:]]
    scratch = [] if has_prev else [pltpu.VMEM((1, A_COLS), F32)]
    return pl.pallas_call(
        functools.partial(_rwkv_pre_kernel, seq_blocks=max(seq_len // tm, 1), has_prev=has_prev),
        grid=(m // tm,), in_specs=in_specs, out_specs=[row(W_A)] * 7,
        out_shape=[jax.ShapeDtypeStruct((m, W_A), F32)] * 7, scratch_shapes=scratch,
        compiler_params=_cparams("arbitrary"), name="rwkv_pre")(*ins)


def _group_norm_gate(y, r, k, v, g, gng, gnb, rk):
    outs = []
    for h in range(N_HEADS):
        sl = slice(h * DH, (h + 1) * DH)
        yh = y[:, sl]
        mean = jnp.mean(yh, axis=-1, keepdims=True)
        var = jnp.mean(jnp.square(yh - mean), axis=-1, keepdims=True)
        yn = (yh - mean) * lax.rsqrt(var + GN_EPS) * gng[:, sl] + gnb[:, sl]
        bonus = jnp.sum(r[:, sl] * k[:, sl] * rk[:, sl], axis=-1, keepdims=True) * v[:, sl]
        outs.append(yn + bonus)
    return jnp.concatenate(outs, axis=-1) * g


def _rwkv_chunk_kernel(r_ref, lw_ref, k_ref, v_ref, kk_ref, a_ref, g_ref, gng_ref, gnb_ref, rk_ref,
                       y_ref, s_ref, s_scr):
    c = pl.program_id(1)
    C = RWKV_CHUNK

    @pl.when(c == 0)
    def _():
        s_scr[...] = jnp.zeros_like(s_scr)

    row = _iota2((C, C), 0)
    col = _iota2((C, C), 1)
    lw = lw_ref[...]
    cum = _ones_dot_left((col <= row).astype(F32), lw)
    e_in = jnp.exp(cum)
    e_ex = jnp.exp(cum - lw)
    e_neg = jnp.exp(-cum)
    r = r_ref[...]
    k = k_ref[...]
    v = v_ref[...]
    kk = kk_ref[...]
    a_hat = -kk * e_ex
    b_hat = kk * a_ref[...] * e_neg
    k_hat = k * e_neg
    r_hat = r * e_in

    n = RWKV_GROUP * C
    gw = RWKV_GROUP * DH
    rb = _iota2((n, gw), 0)
    cb = _iota2((n, gw), 1)
    bdmask = (rb // C) == (cb // DH)
    rb2 = _iota2((n, n), 0) % C
    cb2 = _iota2((n, n), 1) % C
    strict = cb2 < rb2
    lower = cb2 <= rb2

    def bd(x):
        return jnp.where(bdmask, jnp.concatenate([x] * RWKV_GROUP, axis=0), 0.0)

    ys = []
    for gi in range(N_HEADS // RWKV_GROUP):
        sl = slice(gi * gw, (gi + 1) * gw)
        xar = jnp.concatenate([bd(a_hat[:, sl]), bd(r_hat[:, sl])], axis=0).astype(BF16)
        ybk = jnp.concatenate([bd(b_hat[:, sl]), bd(k_hat[:, sl])], axis=0).astype(BF16)
        vbd = bd(v[:, sl]).astype(BF16)
        s0 = s_scr[gi]
        gm = _bdot_nt(xar, ybk)
        xs = _bdot_nt(xar, s0)
        a_ab = jnp.where(strict, gm[:n, :n], 0.0)
        a_ak = jnp.where(strict, gm[:n, n:], 0.0)
        m_r = jnp.concatenate([jnp.where(lower, gm[n:, :n], 0.0), jnp.where(lower, gm[n:, n:], 0.0)], axis=1)
        u = xs[:n] + _bdot(a_ak, vbd)
        pw = a_ab
        u = u + _bdot(pw, u)
        for _ in range(int(math.log2(C)) - 1):
            pw = _bdot(pw, pw)
            u = u + _bdot(pw, u)
        uv = jnp.concatenate([u.astype(BF16), vbd], axis=0)
        y_bd = xs[n:] + _bdot(m_r, uv)
        ys.append(sum(y_bd[hh * C:(hh + 1) * C] for hh in range(RWKV_GROUP)))
        s_scr[gi] = (s0 + _bdot_tn(uv, ybk)) * e_in[C - 1:C, sl]

    y = jnp.concatenate(ys, axis=-1)
    y_ref[...] = _group_norm_gate(y, r, k, v, g_ref[...], gng_ref[...], gnb_ref[...], rk_ref[...])

    @pl.when(c == pl.num_programs(1) - 1)
    def _():
        for h in range(N_HEADS):
            hh = h % RWKV_GROUP
            s_ref[h] = s_scr[h // RWKV_GROUP][hh * DH:(hh + 1) * DH, hh * DH:(hh + 1) * DH]


def rwkv_chunked(pre, batch, seq_len, p):
    C = RWKV_CHUNK
    nc = seq_len // C
    seq = lambda a: a.reshape(batch, seq_len, W_A)
    blk = pl.BlockSpec((None, C, W_A), lambda b, c: (b, c, 0))
    vec = lambda a: a.reshape(1, W_A)
    full = pl.BlockSpec((1, W_A), lambda b, c: (0, 0))
    y, s = pl.pallas_call(
        _rwkv_chunk_kernel, grid=(batch, nc),
        in_specs=[blk] * 7 + [full] * 3,
        out_specs=[blk, pl.BlockSpec((None, N_HEADS, DH, DH), lambda b, c: (b, 0, 0, 0))],
        out_shape=[jax.ShapeDtypeStruct((batch, seq_len, W_A), F32),
                   jax.ShapeDtypeStruct((batch, N_HEADS, DH, DH), F32)],
        scratch_shapes=[pltpu.VMEM((N_HEADS // RWKV_GROUP, RWKV_GROUP * DH, RWKV_GROUP * DH), F32)],
        compiler_params=_cparams("parallel", "arbitrary"), name="rwkv_chunk",
    )(*[seq(a) for a in pre], vec(p['gn_g']), vec(p['gn_b']), vec(p['r_k']))
    return y.reshape(batch * seq_len, W_A), s


def _rwkv_step_kernel(r_ref, lw_ref, k_ref, v_ref, kk_ref, a_ref, g_ref, gng_ref, gnb_ref, rk_ref,
                      s_ref, y_ref, so_ref):
    r = r_ref[...]
    k = k_ref[...]
    v = v_ref[...]
    kk = kk_ref[...]
    w = jnp.exp(lw_ref[...])
    b = kk * a_ref[...]
    eye = _iota2((DH, DH), 0) == _iota2((DH, DH), 1)
    ys = []
    for h in range(N_HEADS):
        sl = slice(h * DH, (h + 1) * DH)
        s = s_ref[h]
        sa = jnp.sum(s * (-kk[:, sl]), axis=-1, keepdims=True)
        v_col = jnp.sum(jnp.where(eye, v[:, sl], 0.0), axis=-1, keepdims=True)
        s = s * w[:, sl] + sa * b[:, sl] + v_col * k[:, sl]
        so_ref[h] = s
        y_col = jnp.sum(s * r[:, sl], axis=-1, keepdims=True)
        ys.append(jnp.sum(jnp.where(eye, y_col, 0.0), axis=0, keepdims=True))
    y = jnp.concatenate(ys, axis=-1)
    y_ref[...] = _group_norm_gate(y, r, k, v, g_ref[...], gng_ref[...], gnb_ref[...], rk_ref[...])


def rwkv_step(pre, state, p):
    batch = state.shape[0]
    tok = lambda a: a.reshape(batch, 1, W_A)
    blk = pl.BlockSpec((None, 1, W_A), lambda b: (b, 0, 0))
    sblk = pl.BlockSpec((None, N_HEADS, DH, DH), lambda b: (b, 0, 0, 0))
    vec = lambda a: a.reshape(1, W_A)
    full = pl.BlockSpec((1, W_A), lambda b: (0, 0))
    y, s = pl.pallas_call(
        _rwkv_step_kernel, grid=(batch,),
        in_specs=[blk] * 7 + [full] * 3 + [sblk], out_specs=[blk, sblk],
        out_shape=[jax.ShapeDtypeStruct((batch, 1, W_A), F32),
                   jax.ShapeDtypeStruct(state.shape, F32)],
        compiler_params=_cparams("parallel"), name="rwkv_step",
    )(*[tok(a) for a in pre], vec(p['gn_g']), vec(p['gn_b']), vec(p['r_k']), state)
    return y.reshape(batch, W_A), s


def _fox_gate_kernel(f_ref, bf_ref, lf_ref, c_ref, ct_ref):
    t = f_ref.shape[0]
    lf = _log_sigmoid(f_ref[...] + bf_ref[...])
    lf_ref[...] = lf[:, :N_HEADS]
    blk = LANE
    row = _iota2((blk, blk), 0)
    col = _iota2((blk, blk), 1)
    tri = (col <= row).astype(F32)
    carry = jnp.zeros((1, LANE), F32)
    for i in range(t // blk):
        c_blk = _ones_dot_left(tri, lf[i * blk:(i + 1) * blk]) + carry
        c_ref[i * blk:(i + 1) * blk, :] = c_blk
        ct_ref[i] = c_blk.T[:N_HEADS]
        carry = c_blk[blk - 1:blk, :]


def fox_gates(f_pad, b_f, batch, seq_len):
    bf = jnp.zeros((1, LANE), F32).at[0, :N_HEADS].set(b_f)
    return pl.pallas_call(
        _fox_gate_kernel, grid=(batch,),
        in_specs=[pl.BlockSpec((None, seq_len, LANE), lambda b: (b, 0, 0)),
                  pl.BlockSpec((1, LANE), lambda b: (0, 0))],
        out_specs=[pl.BlockSpec((None, seq_len, N_HEADS), lambda b: (b, 0, 0)),
                   pl.BlockSpec((None, seq_len, LANE), lambda b: (b, 0, 0)),
                   pl.BlockSpec((None, seq_len // LANE, N_HEADS, LANE), lambda b: (b, 0, 0, 0))],
        out_shape=[jax.ShapeDtypeStruct((batch, seq_len, N_HEADS), F32),
                   jax.ShapeDtypeStruct((batch, seq_len, LANE), F32),
                   jax.ShapeDtypeStruct((batch, seq_len // LANE, N_HEADS, LANE), F32)],
        compiler_params=_cparams("parallel"), name="fox_gates",
    )(f_pad.reshape(batch, seq_len, LANE), bf)


def _fox_prompt_kernel(q_ref, k_ref, v_ref, c_ref, ct_ref, o_ref, kb_scr, vlo_scr, vhi_scr, *, pair_axis):
    pr = pl.program_id(pair_axis)
    qi = pl.program_id(2)
    tq = q_ref.shape[0]
    tk = tq
    lane = _iota2((1, LANE), 1)
    lo = lane < DH

    @pl.when(qi == 0)
    def _():
        kb_scr[...] = k_ref[...].astype(BF16)
        v = v_ref[...]
        vlo_scr[...] = jnp.where(lo, v, 0.0).astype(BF16)
        vhi_scr[...] = jnp.where(lo, 0.0, v).astype(BF16)

    q = q_ref[...] * QK_SCALE
    q_lo = jnp.where(lo, q, 0.0).astype(BF16)
    q_hi = jnp.where(lo, 0.0, q).astype(BF16)
    c_all = c_ref[...]
    head_lane = _iota2(c_all.shape, 1)
    cq = [jnp.sum(jnp.where(head_lane == 2 * pr + j, c_all, 0.0), axis=-1, keepdims=True)
          for j in range(2)]
    qs = (q_lo, q_hi)
    vs = (vlo_scr, vhi_scr)

    def step(j, carry, masked):
        m, l, acc = carry
        off = pl.multiple_of(j * tk, tk)
        kb = kb_scr[pl.ds(off, tk), :]
        new_m, new_l, alphas, pvs = [], [], [], []
        for hh in range(2):
            ck = jnp.concatenate([ct_ref[j * (tk // LANE) + i, pl.ds(2 * pr + hh, 1), :]
                                  for i in range(tk // LANE)], axis=-1)
            s = _bdot_nt(qs[hh], kb) + cq[hh] - ck
            if masked:
                s = jnp.where(_iota2(s.shape, 1) <= _iota2(s.shape, 0), s, NEG_INF)
            mn = jnp.maximum(m[hh], jnp.max(s, axis=-1, keepdims=True))
            alpha = jnp.exp(m[hh] - mn)
            pmat = jnp.exp(s - mn)
            new_m.append(mn)
            new_l.append(l[hh] * alpha + jnp.sum(pmat, axis=-1, keepdims=True))
            alphas.append(alpha)
            pvs.append(jnp.dot(pmat.astype(BF16), vs[hh][pl.ds(off, tk), :], preferred_element_type=F32))
        acc = acc * jnp.where(lo, alphas[0], alphas[1]) + pvs[0] + pvs[1]
        return tuple(new_m), tuple(new_l), acc

    init = ((jnp.full((tq, 1), NEG_INF, F32),) * 2, (jnp.zeros((tq, 1), F32),) * 2,
            jnp.zeros((tq, LANE), F32))
    carry = lax.fori_loop(0, qi, lambda j, cr: step(j, cr, False), init)
    m, l, acc = step(qi, carry, True)
    o_ref[...] = acc / jnp.where(lo, l[0], l[1])


def fox_prompt(q, k, v, c, ct, batch, seq_len):
    tq = min(ATT_BLOCK, seq_len)
    seq = lambda a: a.reshape(batch, seq_len, W_A)
    kv_spec = pl.BlockSpec((None, seq_len, LANE), lambda b, pr, i: (b, 0, pr))
    o = pl.pallas_call(
        functools.partial(_fox_prompt_kernel, pair_axis=1),
        grid=(batch, N_HEADS // 2, seq_len // tq),
        in_specs=[pl.BlockSpec((None, tq, LANE), lambda b, pr, i: (b, i, pr)), kv_spec, kv_spec,
                  pl.BlockSpec((None, tq, LANE), lambda b, pr, i: (b, i, 0)),
                  pl.BlockSpec((None, seq_len // LANE, N_HEADS, LANE), lambda b, pr, i: (b, 0, 0, 0))],
        out_specs=pl.BlockSpec((None, tq, LANE), lambda b, pr, i: (b, i, pr)),
        out_shape=jax.ShapeDtypeStruct((batch, seq_len, W_A), F32),
        scratch_shapes=[pltpu.VMEM((seq_len, LANE), BF16)] * 3,
        compiler_params=_cparams("parallel", "parallel", "arbitrary"), name="fox_prompt",
    )(seq(q), seq(k), seq(v), c, ct)
    return o.reshape(batch * seq_len, W_A)


FOX_PAGES_PER_STEP = 8


def _fox_decode_kernel(pt_ref, q_ref, kn_ref, vn_ref, lfn_ref, *refs, n_pages):
    g = n_pages
    lf_refs = refs[0:g]
    k_refs = refs[g:2 * g]
    v_refs = refs[2 * g:3 * g]
    o_ref = refs[3 * g]
    qb_scr, m_scr, l_scr, acc_scr, tail_scr = refs[3 * g + 1:]
    step = pl.program_id(1)
    page = k_refs[0].shape[-1]
    eye = _iota2((DH, DH), 0) == _iota2((DH, DH), 1)

    @pl.when(step == 0)
    def _():
        m_scr[...] = jnp.full_like(m_scr, NEG_INF)
        l_scr[...] = jnp.zeros_like(l_scr)
        acc_scr[...] = jnp.zeros_like(acc_scr)
        tail_scr[...] = jnp.zeros_like(tail_scr)
        q = q_ref[...] * QK_SCALE
        for h in range(N_HEADS):
            q_col = jnp.sum(jnp.where(eye, q[:, h * DH:(h + 1) * DH], 0.0), axis=-1, keepdims=True)
            qb_scr[h] = jnp.broadcast_to(q_col, (DH, page))

    lfn = lfn_ref[...]
    after = (_iota2((page, page), 0) > _iota2((page, page), 1)).astype(F32)
    tail = tail_scr[...]
    decs = []
    for i in range(g):
        lf = lf_refs[i][...]
        decs.append(_ones_dot_right(lf, after) + tail + lfn)
        tail = tail + jnp.sum(lf, axis=-1, keepdims=True)
    tail_scr[...] = tail
    dec = jnp.concatenate(decs, axis=-1)
    for h in range(N_HEADS):
        qb = qb_scr[h]
        s = jnp.concatenate([jnp.sum(qb * k_refs[i][h], axis=0, keepdims=True) for i in range(g)],
                            axis=-1) + dec[h:h + 1, :]
        m_old = m_scr[h][:, :1]
        mn = jnp.maximum(m_old, jnp.max(s, axis=-1, keepdims=True))
        alpha = jnp.exp(m_old - mn)
        pmat = jnp.exp(s - mn)
        l_new = l_scr[h][:, :1] * alpha + jnp.sum(pmat, axis=-1, keepdims=True)
        m_scr[h] = jnp.broadcast_to(mn, (1, LANE))
        l_scr[h] = jnp.broadcast_to(l_new, (1, LANE))
        acc = acc_scr[h] * alpha
        for i in range(g):
            acc = acc + pmat[:, i * page:(i + 1) * page] * v_refs[i][h]
        acc_scr[h] = acc

    @pl.when(step == pl.num_programs(1) - 1)
    def _():
        q = q_ref[...] * QK_SCALE
        kn = kn_ref[...]
        vn = vn_ref[...]
        outs = []
        for h in range(N_HEADS):
            sl = slice(h * DH, (h + 1) * DH)
            s_new = jnp.sum(q[:, sl] * kn[:, sl], axis=-1, keepdims=True)
            m_old = m_scr[h][:, :1]
            mn = jnp.maximum(m_old, s_new)
            alpha = jnp.exp(m_old - mn)
            p_new = jnp.exp(s_new - mn)
            lt = l_scr[h][:, :1] * alpha + p_new
            o_col = jnp.sum(acc_scr[h], axis=-1, keepdims=True) * alpha
            o_row = jnp.sum(jnp.where(eye, o_col, 0.0), axis=0, keepdims=True)
            outs.append((o_row + p_new * vn[:, sl]) / lt)
        o_ref[...] = jnp.concatenate(outs, axis=-1)


def fox_decode(q, k_new, v_new, logf_new, cache_kt, cache_vt, cache_lf_t, layer, page_table):
    batch, n_pages = page_table.shape
    page = cache_kt.shape[-1]
    g = min(FOX_PAGES_PER_STEP, n_pages)
    steps = n_pages // g
    tok = lambda a: a.reshape(batch, 1, W_A)
    tok_spec = pl.BlockSpec((None, 1, W_A), lambda b, s, pt: (b, 0, 0))

    def page_idx(b, s, pt, i):
        return pt[b * n_pages + (n_pages - 1 - (s * g + i))]

    lf_specs = [pl.BlockSpec((None, None, N_HEADS, page),
                             lambda b, s, pt, i=i: (layer, page_idx(b, s, pt, i), 0, 0)) for i in range(g)]
    kv_specs = [pl.BlockSpec((None, None, N_HEADS, DH, page),
                             lambda b, s, pt, i=i: (layer, page_idx(b, s, pt, i), 0, 0, 0)) for i in range(g)]
    grid_spec = pltpu.PrefetchScalarGridSpec(
        num_scalar_prefetch=1, grid=(batch, steps),
        in_specs=[tok_spec, tok_spec, tok_spec,
                  pl.BlockSpec((None, N_HEADS, 1), lambda b, s, pt: (b, 0, 0))] + lf_specs + kv_specs + kv_specs,
        out_specs=tok_spec,
        scratch_shapes=[pltpu.VMEM((N_HEADS, DH, page), F32), pltpu.VMEM((N_HEADS, 1, LANE), F32),
                        pltpu.VMEM((N_HEADS, 1, LANE), F32), pltpu.VMEM((N_HEADS, DH, page), F32),
                        pltpu.VMEM((N_HEADS, 1), F32)])
    o = pl.pallas_call(
        functools.partial(_fox_decode_kernel, n_pages=g), grid_spec=grid_spec,
        out_shape=jax.ShapeDtypeStruct((batch, 1, W_A), F32),
        compiler_params=_cparams("parallel", "arbitrary"), name="fox_decode",
    )(page_table.reshape(-1), tok(q), tok(k_new), tok(v_new), logf_new.reshape(batch, N_HEADS, 1),
      *([cache_lf_t] * g), *([cache_kt] * g), *([cache_vt] * g))
    return o.reshape(batch, W_A)


def _t5_buckets(dist):
    n = np.maximum(dist, 0)
    max_exact = N_BUCKETS // 2
    nf = np.maximum(n, 1).astype(np.float32)
    large = max_exact + (np.log(nf / max_exact) / math.log(MAX_DISTANCE / max_exact)
                         * (N_BUCKETS - max_exact)).astype(np.int32)
    large = np.minimum(large, N_BUCKETS - 1)
    return np.where(n < max_exact, n, large).astype(np.int32)


def _bias_gather_kernel(idx_ref, table_ref, o_ref):
    col = pl.program_id(0)
    idx = idx_ref[...]
    out = jnp.zeros(idx.shape, F32)
    for bkt in range(N_BUCKETS):
        out = jnp.where(idx == bkt, table_ref[bkt, col], out)
    o_ref[...] = out


def bias_gather(idx, table):
    r, c = idx.shape
    ncol = table.shape[1]
    return pl.pallas_call(
        _bias_gather_kernel, grid=(ncol,),
        in_specs=[pl.BlockSpec((r, c), lambda j: (0, 0)),
                  pl.BlockSpec(memory_space=pltpu.SMEM)],
        out_specs=pl.BlockSpec((None, r, c), lambda j: (j, 0, 0)),
        out_shape=jax.ShapeDtypeStruct((ncol, r, c), F32),
        compiler_params=_cparams("parallel"), name="bias_gather")(idx, table)


def _diff_lambda(lq1, lk1, lq2, lk2, lambda_init):
    return (jnp.exp(jnp.sum(lq1 * lk1, axis=-1, keepdims=True))
            - jnp.exp(jnp.sum(lq2 * lk2, axis=-1, keepdims=True)) + lambda_init)


def _diff_prompt_kernel(q_ref, k_ref, v_ref, bias_ref, far_ref, lam_ref, sub_ref, o_ref, kb_scr, vb_scr,
                        *, lambda_init):
    qi = pl.program_id(2)
    tq = q_ref.shape[0]
    tk = tq
    lane = _iota2((1, LANE), 1)
    lo = lane < DH

    @pl.when(qi == 0)
    def _():
        kb_scr[...] = k_ref[...].astype(BF16)
        vb_scr[...] = v_ref[...].astype(BF16)

    q = q_ref[...] * QK_SCALE
    qs = (jnp.where(lo, q, 0.0).astype(BF16), jnp.where(lo, 0.0, q).astype(BF16))
    head = pl.program_id(1)
    fars = (far_ref[head, 0], far_ref[head, 1])

    def step(j, carry, kind):
        m, l, acc = carry
        off = pl.multiple_of(j * tk, tk)
        kb = kb_scr[pl.ds(off, tk), :]
        vb = vb_scr[pl.ds(off, tk), :]
        new_m, new_l, new_acc = [], [], []
        for mp in range(2):
            s = _bdot_nt(qs[mp], kb)
            if kind == 'far':
                s = s + fars[mp]
            elif kind == 'near':
                s = s + bias_ref[mp, 1]
            else:
                s = s + bias_ref[mp, 0]
                s = jnp.where(_iota2(s.shape, 1) <= _iota2(s.shape, 0), s, NEG_INF)
            mn = jnp.maximum(m[mp], jnp.max(s, axis=-1, keepdims=True))
            alpha = jnp.exp(m[mp] - mn)
            pmat = jnp.exp(s - mn)
            new_m.append(mn)
            new_l.append(l[mp] * alpha + jnp.sum(pmat, axis=-1, keepdims=True))
            new_acc.append(acc[mp] * alpha + jnp.dot(pmat.astype(BF16), vb, preferred_element_type=F32))
        return tuple(new_m), tuple(new_l), tuple(new_acc)

    init = ((jnp.full((tq, 1), NEG_INF, F32),) * 2, (jnp.zeros((tq, 1), F32),) * 2,
            (jnp.zeros((tq, LANE), F32),) * 2)
    carry = lax.fori_loop(0, jnp.maximum(qi - 1, 0), lambda j, cr: step(j, cr, 'far'), init)
    carry = lax.cond(qi >= 1, lambda cr: step(qi - 1, cr, 'near'), lambda cr: cr, carry)
    m, l, acc = step(qi, carry, 'diag')
    lam = _diff_lambda(lam_ref[0:1, :], lam_ref[1:2, :], lam_ref[2:3, :], lam_ref[3:4, :], lambda_init)
    o = acc[0] / l[0] - lam * (acc[1] / l[1])
    o_ref[...] = _rmsnorm(o, sub_ref[...]) * (1.0 - lambda_init)


def diff_prompt(q, k, v, bias_tiles, far_bias, lam_vecs, subln_g, lambda_init, batch, seq_len):
    tq = min(ATT_BLOCK, seq_len)
    seq = lambda a: a.reshape(batch, seq_len, W_C)
    kv_spec = pl.BlockSpec((None, seq_len, LANE), lambda b, h, i: (b, 0, h))
    blk = pl.BlockSpec((None, tq, LANE), lambda b, h, i: (b, i, h))
    o = pl.pallas_call(
        functools.partial(_diff_prompt_kernel, lambda_init=lambda_init),
        grid=(batch, N_HEADS, seq_len // tq),
        in_specs=[blk, kv_spec, kv_spec,
                  pl.BlockSpec((None, 2, 2, tq, tq), lambda b, h, i: (h, 0, 0, 0, 0)),
                  pl.BlockSpec(memory_space=pltpu.SMEM),
                  pl.BlockSpec((4, DH), lambda b, h, i: (0, 0)),
                  pl.BlockSpec((1, LANE), lambda b, h, i: (0, 0))],
        out_specs=blk,
        out_shape=jax.ShapeDtypeStruct((batch, seq_len, W_C), F32),
        scratch_shapes=[pltpu.VMEM((seq_len, LANE), BF16)] * 2,
        compiler_params=_cparams("parallel", "parallel", "arbitrary"), name="diff_prompt",
    )(seq(q), seq(k), seq(v), bias_tiles, far_bias, lam_vecs, subln_g.reshape(1, LANE))
    return o.reshape(batch * seq_len, W_C)


DIFF_PAGES_PER_STEP = 8


def _diff_decode_kernel(pt_ref, q_ref, kn_ref, vn_ref, bias_ref, rel_ref, lam_ref, sub_ref, *refs,
                        n_pages, lambda_init):
    g = n_pages
    k_refs = refs[0:g]
    v_refs = refs[g:2 * g]
    o_ref = refs[2 * g]
    m_scr, l_scr, acc_scr = refs[2 * g + 1:]
    step = pl.program_id(1)

    @pl.when(step == 0)
    def _():
        m_scr[...] = jnp.full_like(m_scr, NEG_INF)
        l_scr[...] = jnp.zeros_like(l_scr)
        acc_scr[...] = jnp.zeros_like(acc_scr)

    row = _iota2((N_HEADS, LANE), 0)
    lane = _iota2((N_HEADS, LANE), 1)
    qsel = ((row == 0) & (lane < DH)) | ((row == 1) & (lane >= DH))

    def q_tile(h):
        return jnp.where(qsel, q_ref[h:h + 1, :] * QK_SCALE, 0.0)

    page = k_refs[0].shape[0] // N_HEADS
    for h in range(N_HEADS):
        rows_h = pl.ds(h, page, stride=N_HEADS)
        kh = jnp.concatenate([k_refs[i][rows_h, :] for i in range(g)], axis=0)
        vh = jnp.concatenate([v_refs[i][rows_h, :] for i in range(g)], axis=0)
        s = lax.dot_general(q_tile(h), kh, (((1,), (1,)), ((), ())),
                            preferred_element_type=F32) + bias_ref[h]
        m_old = m_scr[h][:, :1]
        mn = jnp.maximum(m_old, jnp.max(s, axis=-1, keepdims=True))
        alpha = jnp.exp(m_old - mn)
        pmat = jnp.exp(s - mn)
        l_new = l_scr[h][:, :1] * alpha + jnp.sum(pmat, axis=-1, keepdims=True)
        m_scr[h] = jnp.broadcast_to(mn, (N_HEADS, LANE))
        l_scr[h] = jnp.broadcast_to(l_new, (N_HEADS, LANE))
        acc_scr[h] = acc_scr[h] * alpha + jnp.dot(pmat, vh, preferred_element_type=F32)

    @pl.when(step == pl.num_programs(1) - 1)
    def _():
        lam = _diff_lambda(lam_ref[0:1, :], lam_ref[1:2, :], lam_ref[2:3, :], lam_ref[3:4, :], lambda_init)
        for h in range(N_HEADS):
            b_new = jnp.where(row[:, :1] == 0, rel_ref[0, 2 * h], rel_ref[0, 2 * h + 1])
            s_new = jnp.sum(q_tile(h) * kn_ref[h:h + 1, :], axis=-1, keepdims=True) + b_new
            m_old = m_scr[h][:, :1]
            mn = jnp.maximum(m_old, s_new)
            alpha = jnp.exp(m_old - mn)
            p_new = jnp.exp(s_new - mn)
            lt = l_scr[h][:, :1] * alpha + p_new
            o = (acc_scr[h] * alpha + p_new * vn_ref[h:h + 1, :]) / lt
            od = o[0:1] - lam * o[1:2]
            od = od * lax.rsqrt(jnp.mean(od * od, axis=-1, keepdims=True) + RMS_EPS)
            o_ref[:, h * LANE:(h + 1) * LANE] = od * sub_ref[...] * (1.0 - lambda_init)


def diff_decode(q, k_new, v_new, cache_k, cache_v, layer, page_table, page_bias, rel_bias, lam_vecs,
                subln_g, lambda_init):
    batch, n_pages = page_table.shape
    page = cache_k.shape[2]
    g = min(DIFF_PAGES_PER_STEP, n_pages)
    steps = n_pages // g
    heads = lambda a: a.reshape(batch, N_HEADS, LANE)
    head_spec = pl.BlockSpec((None, N_HEADS, LANE), lambda b, s, pt: (b, 0, 0))
    pool = cache_k.shape[1]
    cache_k = cache_k.reshape(-1, pool, page * N_HEADS, LANE)
    cache_v = cache_v.reshape(-1, pool, page * N_HEADS, LANE)
    kv_specs = [pl.BlockSpec((None, None, page * N_HEADS, LANE),
                             lambda b, s, pt, i=i: (layer, pt[b * n_pages + s * g + i], 0, 0))
                for i in range(g)]
    grid_spec = pltpu.PrefetchScalarGridSpec(
        num_scalar_prefetch=1, grid=(batch, steps),
        in_specs=[head_spec, head_spec, head_spec,
                  pl.BlockSpec((None, N_HEADS, N_HEADS, g * page), lambda b, s, pt: (s, 0, 0, 0)),
                  pl.BlockSpec(memory_space=pltpu.SMEM),
                  pl.BlockSpec((4, DH), lambda b, s, pt: (0, 0)),
                  pl.BlockSpec((1, LANE), lambda b, s, pt: (0, 0))] + kv_specs + kv_specs,
        out_specs=pl.BlockSpec((None, 1, W_C), lambda b, s, pt: (b, 0, 0)),
        scratch_shapes=[pltpu.VMEM((N_HEADS, N_HEADS, LANE), F32)] * 3)
    o = pl.pallas_call(
        functools.partial(_diff_decode_kernel, n_pages=g, lambda_init=lambda_init), grid_spec=grid_spec,
        out_shape=jax.ShapeDtypeStruct((batch, 1, W_C), F32),
        compiler_params=_cparams("parallel", "arbitrary"), name="diff_decode",
    )(page_table.reshape(-1), heads(q), heads(k_new), heads(v_new), page_bias, rel_bias, lam_vecs,
      subln_g.reshape(1, LANE), *([cache_k] * g), *([cache_v] * g))
    return o.reshape(batch, W_C)


def _prep_params(norm_mix, norm_ffn, norm_final, w_in_ab, w_out_ab, rwkv_mu, rwkv_w0, rwkv_w_up, rwkv_a0,
                 rwkv_a_up, rwkv_g_up, rwkv_k_k, rwkv_k_a, rwkv_r_k, rwkv_gn_g, rwkv_gn_b, fox_b_f, w_in_c,
                 w_out_c, diff_lq1, diff_lk1, diff_lq2, diff_lk2, diff_subln, rel_bias, ffn_gate, ffn_up,
                 ffn_down):
    depth = norm_mix.shape[0]
    o_r, o_wd, o_k, o_v, o_ad, o_gd = np.cumsum([0, W_A, LORA_W, W_A, W_A, LORA_A]).tolist()
    perm = np.concatenate([np.arange(o_r, o_r + W_A), np.arange(o_k, o_k + W_A), np.arange(o_v, o_v + W_A),
                           np.arange(o_wd, o_wd + LORA_W), np.arange(o_ad, o_ad + LORA_A),
                           np.arange(o_gd, o_gd + LORA_G)])
    layers = []
    for li in range(depth):
        j = li // 2
        lp = {'norm_mix': norm_mix[li], 'norm_ffn': norm_ffn[li],
              'ffn_gate': ffn_gate[li].astype(BF16), 'ffn_up': ffn_up[li].astype(BF16),
              'ffn_down': ffn_down[li].astype(BF16)}
        if li % 2 == 0:
            w = w_in_ab[j]
            wa = w[:, :A_COLS][:, perm]
            wb = w[:, A_COLS:]
            wf = jnp.pad(wb[:, 3 * W_A:], ((0, 0), (0, LANE - N_HEADS)))
            lp['w_in'] = [wa.astype(BF16), wb[:, :W_A].astype(BF16), wb[:, W_A:2 * W_A].astype(BF16),
                          wb[:, 2 * W_A:3 * W_A].astype(BF16), wf.astype(BF16)]
            lp['w_out'] = [w_out_ab[j][:W_A].astype(BF16), w_out_ab[j][W_A:].astype(BF16)]
            zeros = jnp.zeros((LORA_W, W_A), F32)
            w_lora = jnp.concatenate([jnp.concatenate([rwkv_w_up[j], zeros], axis=1),
                                      jnp.concatenate([zeros, rwkv_a_up[j]], axis=1)], axis=0)
            lp['rwkv'] = {'mu': rwkv_mu[j][perm], 'w_lora': w_lora.astype(BF16),
                          'g_up': rwkv_g_up[j].astype(BF16), 'w0': rwkv_w0[j], 'a0': rwkv_a0[j],
                          'k_k': rwkv_k_k[j], 'k_a': rwkv_k_a[j], 'r_k': rwkv_r_k[j].reshape(-1),
                          'gn_g': rwkv_gn_g[j], 'gn_b': rwkv_gn_b[j]}
            lp['fox_b_f'] = fox_b_f[j]
        else:
            w = w_in_c[j]
            lp['w_in'] = [w[:, :W_C].astype(BF16), w[:, W_C:2 * W_C].astype(BF16), w[:, 2 * W_C:].astype(BF16)]
            lp['w_out'] = [w_out_c[j].astype(BF16)]
            lp['lam_vecs'] = jnp.stack([diff_lq1[j], diff_lk1[j], diff_lq2[j], diff_lk2[j]])
            lp['subln'] = diff_subln[j]
            lp['lambda_init'] = 0.8 - 0.6 * math.exp(-0.3 * li)
        layers.append(lp)
    return layers, norm_final, rel_bias


def _prompt_bias(rel_bias, seq_len):
    tq = min(ATT_BLOCK, seq_len)
    d = np.arange(tq)[:, None] - np.arange(tq)[None, :]
    idx = np.concatenate([_t5_buckets(d), _t5_buckets(d + tq)], axis=0)
    tiles = bias_gather(jnp.asarray(idx), rel_bias)
    tiles = tiles.reshape(N_HEADS, 2, 2, tq, tq)
    far_idx = np.full((8, LANE), int(_t5_buckets(np.array([2 * tq]))[0]), np.int32)
    far = bias_gather(jnp.asarray(far_idx), rel_bias)[:, 0, 0].reshape(N_HEADS, 2)
    return tiles, far


def _decode_bias(rel_bias, n_pages, page, g):
    t0 = n_pages * page
    steps = n_pages // g
    rows = -(-steps // 8) * 8
    idx = np.zeros((rows, g * page), np.int32)
    idx[:steps] = _t5_buckets(t0 - np.arange(t0).reshape(steps, g * page))
    bias = bias_gather(jnp.asarray(idx), rel_bias)[:, :steps]
    bias = jnp.transpose(bias.reshape(N_HEADS, 2, steps, g * page), (2, 0, 1, 3))
    return jnp.pad(bias, ((0, 0), (0, 0), (0, N_HEADS - 2), (0, 0)))


def _decode_logf_kernel(f_ref, bf_ref, o_ref):
    o_ref[...] = _log_sigmoid(f_ref[...] + bf_ref[...])[:, :N_HEADS]


def _decode_logf(f_pad, b_f):
    bf = jnp.zeros((1, LANE), F32).at[0, :N_HEADS].set(b_f)
    return pl.pallas_call(
        _decode_logf_kernel, out_shape=jax.ShapeDtypeStruct((f_pad.shape[0], N_HEADS), F32),
        name="decode_logf")(f_pad, bf)


def _trunk(x, layers, norm_final, batch, seq_len, past, prompt_bias=None):
    fk, fv, fl, dk, dv, rs, sh = [], [], [], [], [], [], []
    y = None
    for li, lp in enumerate(layers):
        j = li // 2
        if li % 2 == 0:
            pa, q, k, v, f_pad, xn = norm_proj(x, lp['norm_mix'], lp['w_in'], emit_xn=True)
            if past is None:
                pre = rwkv_pre(pa, None, seq_len, lp['rwkv'])
                ya, s_fin = rwkv_chunked(pre, batch, seq_len, lp['rwkv'])
                logf, c, ct = fox_gates(f_pad, lp['fox_b_f'], batch, seq_len)
                yb = fox_prompt(q, k, v, c, ct, batch, seq_len)
            else:
                (pa_prev,) = norm_proj(past['state_shift'][j], None, lp['w_in'][:1])
                pre = rwkv_pre(pa, pa_prev, seq_len, lp['rwkv'])
                ya, s_fin = rwkv_step(pre, past['state_rwkv'][j], lp['rwkv'])
                logf = _decode_logf(f_pad, lp['fox_b_f'])
                yb = fox_decode(q, k, v, logf, past['cache_fox_kt'], past['cache_fox_vt'],
                                past['cache_fox_logf_t'], j, past['page_table'])
                logf = logf.reshape(batch, 1, N_HEADS)
            x = proj_residual(x, [ya, yb], lp['w_out'])
            fk.append(k.reshape(batch, seq_len, N_HEADS, DH))
            fv.append(v.reshape(batch, seq_len, N_HEADS, DH))
            fl.append(logf)
            rs.append(s_fin)
            sh.append(xn.reshape(batch, seq_len, D_MODEL)[:, -1])
        else:
            q, k, v = norm_proj(x, lp['norm_mix'], lp['w_in'])
            if past is None:
                yc = diff_prompt(q, k, v, prompt_bias[0], prompt_bias[1], lp['lam_vecs'],
                                 lp['subln'], lp['lambda_init'], batch, seq_len)
            else:
                yc = diff_decode(q, k, v, past['cache_diff_k'], past['cache_diff_v'], j, past['page_table'],
                                 past['page_bias'], past['rel_bias'], lp['lam_vecs'], lp['subln'],
                                 lp['lambda_init'])
            x = proj_residual(x, [yc], lp['w_out'])
            dk.append(k.reshape(batch, seq_len, N_HEADS, 2 * DH))
            dv.append(v.reshape(batch, seq_len, N_HEADS, 2 * DH))
        if li == len(layers) - 1:
            x, y = ffn_residual(x, lp['norm_ffn'], lp['ffn_gate'], lp['ffn_up'], lp['ffn_down'], norm_final)
        else:
            x = ffn_residual(x, lp['norm_ffn'], lp['ffn_gate'], lp['ffn_up'], lp['ffn_down'])
    return (y.reshape(batch, seq_len, D_MODEL), jnp.stack(fk), jnp.stack(fv), jnp.stack(fl), jnp.stack(dk),
            jnp.stack(dv), jnp.stack(rs), jnp.stack(sh))


def kernel(x_prompt, x_sample, cache_fox_k, cache_fox_v, cache_fox_logf, cache_diff_k, cache_diff_v, state_rwkv, state_shift, page_table, norm_mix, norm_ffn, norm_final, w_in_ab, w_out_ab, rwkv_mu, rwkv_w0, rwkv_w_up, rwkv_a0, rwkv_a_up, rwkv_g_up, rwkv_k_k, rwkv_k_a, rwkv_r_k, rwkv_gn_g, rwkv_gn_b, fox_b_f, w_in_c, w_out_c, diff_lq1, diff_lk1, diff_lq2, diff_lk2, diff_subln, rel_bias, ffn_gate, ffn_up, ffn_down):
    layers, norm_final, rel_bias = _prep_params(
        norm_mix, norm_ffn, norm_final, w_in_ab, w_out_ab, rwkv_mu, rwkv_w0, rwkv_w_up, rwkv_a0, rwkv_a_up,
        rwkv_g_up, rwkv_k_k, rwkv_k_a, rwkv_r_k, rwkv_gn_g, rwkv_gn_b, fox_b_f, w_in_c, w_out_c, diff_lq1,
        diff_lk1, diff_lq2, diff_lk2, diff_subln, rel_bias, ffn_gate, ffn_up, ffn_down)
    bp, tp, d = x_prompt.shape
    bs, ts, _ = x_sample.shape
    n_pages = page_table.shape[1]
    page = cache_fox_k.shape[2]
    past = {
        'state_shift': state_shift, 'state_rwkv': state_rwkv, 'page_table': page_table,
        'cache_fox_kt': jnp.transpose(cache_fox_k, (0, 1, 3, 4, 2)),
        'cache_fox_vt': jnp.transpose(cache_fox_v, (0, 1, 3, 4, 2)),
        'cache_fox_logf_t': jnp.swapaxes(cache_fox_logf, 2, 3),
        'cache_diff_k': cache_diff_k, 'cache_diff_v': cache_diff_v, 'rel_bias': rel_bias,
        'page_bias': _decode_bias(rel_bias, n_pages, page, min(DIFF_PAGES_PER_STEP, n_pages)),
    }
    out_p = _trunk(x_prompt.reshape(bp * tp, d), layers, norm_final, bp, tp, None,
                   _prompt_bias(rel_bias, tp))
    out_s = _trunk(x_sample.reshape(bs * ts, d), layers, norm_final, bs, ts, past)
    return (out_p[0], out_s[0]) + out_p[1:] + out_s[1:]
```

```python
import functools
import math

import jax
import jax.numpy as jnp
import numpy as np
from jax import lax
from jax.experimental import pallas as pl
from jax.experimental.pallas import tpu as pltpu

F32 = jnp.float32
BF16 = jnp.bfloat16

D_MODEL = 1024
N_HEADS = 8
DH = 64
W_A = N_HEADS * DH
LORA_W = 64
LORA_A = 64
LORA_G = 128
A_COLS = 3 * W_A + LORA_W + LORA_A + LORA_G
W_C = N_HEADS * 2 * DH
N_BUCKETS = 32
MAX_DISTANCE = 128
RMS_EPS = 1e-6
GN_EPS = 64e-5
NEG_INF = -1e30
QK_SCALE = DH ** -0.5

LANE = 128
RWKV_CHUNK = 64
RWKV_GROUP = 4
RWKV_SEQS = 4
ATT_BLOCK = 512
VMEM_LIMIT = 48 * 1024 * 1024


def _cparams(*sem):
    return pltpu.CompilerParams(dimension_semantics=sem, vmem_limit_bytes=VMEM_LIMIT)


def _bdot(a, b):
    return jnp.dot(a.astype(BF16), b.astype(BF16), preferred_element_type=F32)


def _bdot_nt(a, b):
    return lax.dot_general(a.astype(BF16), b.astype(BF16), (((1,), (1,)), ((), ())),
                           preferred_element_type=F32)


def _bdot_tn(a, b):
    return lax.dot_general(a.astype(BF16), b.astype(BF16), (((0,), (0,)), ((), ())),
                           preferred_element_type=F32)


def _split3(x):
    hi = x.astype(BF16).astype(F32)
    r1 = x - hi
    mid = r1.astype(BF16).astype(F32)
    lo = (r1 - mid).astype(BF16).astype(F32)
    return hi, mid, lo


def _ones_dot_left(ones_mat, x):
    n = x.shape[1]
    stack = jnp.concatenate(_split3(x), axis=1).astype(BF16)
    out = jnp.dot(ones_mat.astype(BF16), stack, preferred_element_type=F32)
    return out[:, :n] + out[:, n:2 * n] + out[:, 2 * n:]


def _ones_dot_right(x, ones_mat):
    m = x.shape[0]
    stack = jnp.concatenate(_split3(x), axis=0).astype(BF16)
    out = jnp.dot(stack, ones_mat.astype(BF16), preferred_element_type=F32)
    return out[:m] + out[m:2 * m] + out[2 * m:]


def _sigmoid(x):
    return 1.0 / (1.0 + jnp.exp(-x))


def _log_sigmoid(x):
    return jnp.minimum(x, 0.0) - jnp.log(1.0 + jnp.exp(-jnp.abs(x)))


def _softplus(x):
    return jnp.maximum(x, 0.0) + jnp.log(1.0 + jnp.exp(-jnp.abs(x)))


def _rmsnorm(x, g):
    return x * lax.rsqrt(jnp.mean(x * x, axis=-1, keepdims=True) + RMS_EPS) * g


def _iota2(shape, dim):
    return lax.broadcasted_iota(jnp.int32, shape, dim)


def _row_tile(m):
    for t in (512, 256, 128):
        if m % t == 0:
            return t
    return m


def _norm_proj_kernel(*refs, n_out, with_norm, emit_xn):
    x_ref = refs[0]
    pos = 1
    if with_norm:
        g_ref = refs[pos]
        pos += 1
    w_refs = refs[pos:pos + n_out]
    pos += n_out
    out_refs = refs[pos:pos + n_out]
    pos += n_out
    x = x_ref[...]
    if with_norm:
        x = _rmsnorm(x, g_ref[...])
        if emit_xn:
            refs[pos][...] = x
    xb = x.astype(BF16)
    for w_ref, o_ref in zip(w_refs, out_refs):
        o_ref[...] = jnp.dot(xb, w_ref[...], preferred_element_type=F32)


def norm_proj(x, g, ws, emit_xn=False):
    m, d = x.shape
    tm = _row_tile(m)
    with_norm = g is not None
    ins = [x]
    in_specs = [pl.BlockSpec((tm, d), lambda i: (i, 0))]
    if with_norm:
        ins.append(g.reshape(1, d))
        in_specs.append(pl.BlockSpec((1, d), lambda i: (0, 0)))
    for w in ws:
        ins.append(w)
        in_specs.append(pl.BlockSpec(w.shape, lambda i: (0, 0)))
    out_shape = [jax.ShapeDtypeStruct((m, w.shape[1]), F32) for w in ws]
    out_specs = [pl.BlockSpec((tm, w.shape[1]), lambda i: (i, 0)) for w in ws]
    if emit_xn:
        out_shape.append(jax.ShapeDtypeStruct((m, d), F32))
        out_specs.append(pl.BlockSpec((tm, d), lambda i: (i, 0)))
    return pl.pallas_call(
        functools.partial(_norm_proj_kernel, n_out=len(ws), with_norm=with_norm, emit_xn=emit_xn),
        grid=(m // tm,), in_specs=in_specs, out_specs=out_specs, out_shape=out_shape,
        compiler_params=_cparams("parallel"), name="norm_proj")(*ins)


def _proj_residual_kernel(*refs, n_in):
    x_ref = refs[0]
    y_refs = refs[1:1 + n_in]
    w_refs = refs[1 + n_in:1 + 2 * n_in]
    o_ref = refs[1 + 2 * n_in]
    acc = x_ref[...]
    for y_ref, w_ref in zip(y_refs, w_refs):
        acc = acc + jnp.dot(y_ref[...].astype(BF16), w_ref[...], preferred_element_type=F32)
    o_ref[...] = acc


def proj_residual(x, ys, ws):
    m, d = x.shape
    tm = _row_tile(m)
    in_specs = [pl.BlockSpec((tm, d), lambda i: (i, 0))]
    in_specs += [pl.BlockSpec((tm, y.shape[1]), lambda i: (i, 0)) for y in ys]
    in_specs += [pl.BlockSpec(w.shape, lambda i: (0, 0)) for w in ws]
    return pl.pallas_call(
        functools.partial(_proj_residual_kernel, n_in=len(ys)),
        grid=(m // tm,), in_specs=in_specs, out_specs=pl.BlockSpec((tm, d), lambda i: (i, 0)),
        out_shape=jax.ShapeDtypeStruct((m, d), F32),
        compiler_params=_cparams("parallel"), name="proj_residual")(x, *ys, *ws)


FFN_CHUNKS = 2


def _ffn_kernel(*refs, final_norm):
    if final_norm:
        x_ref, g_ref, wg_ref, wu_ref, wd_ref, gf_ref, o_ref, y_ref = refs
    else:
        x_ref, g_ref, wg_ref, wu_ref, wd_ref, o_ref = refs
    x = x_ref[...]
    xn = _rmsnorm(x, g_ref[...]).astype(BF16)
    fc = wg_ref.shape[1] // FFN_CHUNKS
    out = x
    for c in range(FFN_CHUNKS):
        cols = slice(c * fc, (c + 1) * fc)
        gate = jnp.dot(xn, wg_ref[:, cols], preferred_element_type=F32)
        up = jnp.dot(xn, wu_ref[:, cols], preferred_element_type=F32)
        h = (gate * _sigmoid(gate) * up).astype(BF16)
        out = out + jnp.dot(h, wd_ref[cols, :], preferred_element_type=F32)
    o_ref[...] = out
    if final_norm:
        y_ref[...] = _rmsnorm(out, gf_ref[...])


def ffn_residual(x, g, wg, wu, wd, final_g=None):
    m, d = x.shape
    tm = _row_tile(m)
    final_norm = final_g is not None
    resident = lambda a: pl.BlockSpec(a.shape, lambda i: (0, 0), pipeline_mode=pl.Buffered(1))
    row = pl.BlockSpec((tm, d), lambda i: (i, 0))
    vec = pl.BlockSpec((1, d), lambda i: (0, 0))
    ins = [x, g.reshape(1, d), wg, wu, wd]
    in_specs = [row, vec, resident(wg), resident(wu), resident(wd)]
    out_shape = [jax.ShapeDtypeStruct((m, d), F32)]
    out_specs = [row]
    if final_norm:
        ins.append(final_g.reshape(1, d))
        in_specs.append(vec)
        out_shape.append(jax.ShapeDtypeStruct((m, d), F32))
        out_specs.append(row)
    res = pl.pallas_call(
        functools.partial(_ffn_kernel, final_norm=final_norm),
        grid=(m // tm,), in_specs=in_specs, out_specs=out_specs, out_shape=out_shape,
        compiler_params=_cparams("parallel"), name="ffn")(*ins)
    return res if final_norm else res[0]


def _head_ones():
    r = _iota2((W_A, W_A), 0) // DH
    c = _iota2((W_A, W_A), 1) // DH
    return (r == c).astype(F32)


def _rwkv_pre_kernel(*refs, seq_blocks, has_prev):
    if has_prev:
        (pa_ref, prev_ref, mu_ref, wlora_ref, gup_ref, w0_ref, a0_ref, kk_ref, ka_ref,
         r_out, lw_out, k_out, v_out, kkn_out, a_out, g_out) = refs
    else:
        (pa_ref, mu_ref, wlora_ref, gup_ref, w0_ref, a0_ref, kk_ref, ka_ref,
         r_out, lw_out, k_out, v_out, kkn_out, a_out, g_out, carry) = refs
    pa = pa_ref[...]
    tm = pa.shape[0]
    if has_prev:
        shifted = prev_ref[...]
    else:
        i = pl.program_id(0)
        first = (i % seq_blocks) == 0
        prev_row = jnp.where(first, 0.0, carry[...])
        rolled = pltpu.roll(pa, 1, 0)
        shifted = jnp.where(_iota2(pa.shape, 0) == 0, prev_row, rolled)
        carry[...] = pa[tm - 1:tm, :]
    z = pa + (shifted - pa) * mu_ref[...]
    r = z[:, 0:W_A]
    k = z[:, W_A:2 * W_A]
    v = z[:, 2 * W_A:3 * W_A]
    lo = z[:, 3 * W_A:3 * W_A + LORA_W + LORA_A]
    gd = z[:, 3 * W_A + LORA_W + LORA_A:]
    lane = _iota2(lo.shape, 1)
    lo = jnp.where(lane < LORA_W, jnp.tanh(lo), lo)
    up = _bdot(lo, wlora_ref[...])
    w = -_softplus(-(w0_ref[...] + up[:, :W_A])) - 0.5
    a = _sigmoid(a0_ref[...] + up[:, W_A:])
    g = _bdot(_sigmoid(gd), gup_ref[...])
    kk = k * kk_ref[...]
    ss = _ones_dot_right(kk * kk, _head_ones())
    kk = kk / jnp.maximum(jnp.sqrt(ss), 1e-12)
    r_out[...] = r
    lw_out[...] = -jnp.exp(w)
    k_out[...] = k * (1.0 + (a - 1.0) * ka_ref[...])
    v_out[...] = v
    kkn_out[...] = kk
    a_out[...] = a
    g_out[...] = g


def rwkv_pre(pa, pa_prev, seq_len, p):
    m = pa.shape[0]
    has_prev = pa_prev is not None
    tm = m if has_prev else min(512, seq_len)
    vec = lambda a: a.reshape(1, -1)
    ins = [pa] + ([pa_prev] if has_prev else []) + [
        vec(p['mu']), p['w_lora'], p['g_up'], vec(p['w0']), vec(p['a0']), vec(p['k_k']), vec(p['k_a'])]
    row = lambda n: pl.BlockSpec((tm, n), lambda i: (i, 0))
    full = lambda a: pl.BlockSpec(a.shape, lambda i: (0, 0))
    in_specs = [row(A_COLS)] + ([row(A_COLS)] if has_prev else []) + [full(a) for a in ins[-7:]]
    scratch = [] if has_prev else [pltpu.VMEM((1, A_COLS), F32)]
    return pl.pallas_call(
        functools.partial(_rwkv_pre_kernel, seq_blocks=max(seq_len // tm, 1), has_prev=has_prev),
        grid=(m // tm,), in_specs=in_specs, out_specs=[row(W_A)] * 7,
        out_shape=[jax.ShapeDtypeStruct((m, W_A), F32)] * 7, scratch_shapes=scratch,
        compiler_params=_cparams("arbitrary"), name="rwkv_pre")(*ins)


def _group_norm_gate(y, r, k, v, g, gng, gnb, rk):
    outs = []
    for h in range(N_HEADS):
        sl = slice(h * DH, (h + 1) * DH)
        yh = y[:, sl]
        mean = jnp.mean(yh, axis=-1, keepdims=True)
        var = jnp.mean(jnp.square(yh - mean), axis=-1, keepdims=True)
        yn = (yh - mean) * lax.rsqrt(var + GN_EPS) * gng[:, sl] + gnb[:, sl]
        bonus = jnp.sum(r[:, sl] * k[:, sl] * rk[:, sl], axis=-1, keepdims=True) * v[:, sl]
        outs.append(yn + bonus)
    return jnp.concatenate(outs, axis=-1) * g


def _rwkv_chunk_kernel(r_ref, lw_ref, k_ref, v_ref, kk_ref, a_ref, g_ref, gng_ref, gnb_ref, rk_ref,
                       y_ref, s_ref, s_scr):
    c = pl.program_id(1)
    C = RWKV_CHUNK

    @pl.when(c == 0)
    def _():
        s_scr[...] = jnp.zeros_like(s_scr)

    tri = (_iota2((C, C), 1) <= _iota2((C, C), 0)).astype(F32)

    n = RWKV_GROUP * C
    gw = RWKV_GROUP * DH
    rb = _iota2((n, gw), 0)
    cb = _iota2((n, gw), 1)
    bdmask = (rb // C) == (cb // DH)
    rb2 = _iota2((n, n), 0) % C
    cb2 = _iota2((n, n), 1) % C
    strict = cb2 < rb2
    lower = cb2 <= rb2

    def bd(x):
        return jnp.where(bdmask, jnp.concatenate([x] * RWKV_GROUP, axis=0), 0.0)

    nb = r_ref.shape[0]
    ng = N_HEADS // RWKV_GROUP
    chains = [(bi, gi) for bi in range(nb) for gi in range(ng)]
    e_last, xar, ybk, vbd, s0 = {}, {}, {}, {}, {}
    for bi in range(nb):
        lw = lw_ref[bi]
        cum = _ones_dot_left(tri, lw)
        e_in = jnp.exp(cum)
        e_neg = jnp.exp(-cum)
        kk = kk_ref[bi]
        a_hat = -kk * jnp.exp(cum - lw)
        b_hat = kk * a_ref[bi] * e_neg
        k_hat = k_ref[bi] * e_neg
        r_hat = r_ref[bi] * e_in
        v = v_ref[bi]
        e_last[bi] = e_in[C - 1:C, :]
        for gi in range(ng):
            sl = slice(gi * gw, (gi + 1) * gw)
            xar[bi, gi] = jnp.concatenate([bd(a_hat[:, sl]), bd(r_hat[:, sl])], axis=0).astype(BF16)
            ybk[bi, gi] = jnp.concatenate([bd(b_hat[:, sl]), bd(k_hat[:, sl])], axis=0).astype(BF16)
            vbd[bi, gi] = bd(v[:, sl]).astype(BF16)
            s0[bi, gi] = s_scr[bi, gi]
    gm = {ch: _bdot_nt(xar[ch], ybk[ch]) for ch in chains}
    xs = {ch: _bdot_nt(xar[ch], s0[ch]) for ch in chains}
    pw = {ch: jnp.where(strict, gm[ch][:n, :n], 0.0) for ch in chains}
    u = {ch: xs[ch][:n] + _bdot(jnp.where(strict, gm[ch][:n, n:], 0.0), vbd[ch]) for ch in chains}
    u = {ch: u[ch] + _bdot(pw[ch], u[ch]) for ch in chains}
    for _ in range(int(math.log2(C)) - 1):
        pw = {ch: _bdot(pw[ch], pw[ch]) for ch in chains}
        u = {ch: u[ch] + _bdot(pw[ch], u[ch]) for ch in chains}
    uv = {ch: jnp.concatenate([u[ch].astype(BF16), vbd[ch]], axis=0) for ch in chains}
    ys = {}
    for ch in chains:
        m_r = jnp.concatenate([jnp.where(lower, gm[ch][n:, :n], 0.0), jnp.where(lower, gm[ch][n:, n:], 0.0)],
                              axis=1)
        y_bd = xs[ch][n:] + _bdot(m_r, uv[ch])
        ys[ch] = sum(y_bd[hh * C:(hh + 1) * C] for hh in range(RWKV_GROUP))
    for bi, gi in chains:
        s_scr[bi, gi] = ((s0[bi, gi] + _bdot_tn(uv[bi, gi], ybk[bi, gi]))
                         * e_last[bi][:, gi * gw:(gi + 1) * gw])
    for bi in range(nb):
        y = jnp.concatenate([ys[bi, gi] for gi in range(ng)], axis=-1)
        y_ref[bi] = _group_norm_gate(y, r_ref[bi], k_ref[bi], v_ref[bi], g_ref[bi], gng_ref[...],
                                     gnb_ref[...], rk_ref[...])

    @pl.when(c == pl.num_programs(1) - 1)
    def _():
        for bi in range(r_ref.shape[0]):
            for h in range(N_HEADS):
                hh = h % RWKV_GROUP
                s_ref[bi, h] = s_scr[bi, h // RWKV_GROUP][hh * DH:(hh + 1) * DH, hh * DH:(hh + 1) * DH]


def rwkv_chunked(pre, batch, seq_len, p):
    C = RWKV_CHUNK
    nc = seq_len // C
    nb = RWKV_SEQS if batch % RWKV_SEQS == 0 else 1
    seq = lambda a: a.reshape(batch, seq_len, W_A)
    blk = pl.BlockSpec((nb, C, W_A), lambda b, c: (b, c, 0))
    vec = lambda a: a.reshape(1, W_A)
    full = pl.BlockSpec((1, W_A), lambda b, c: (0, 0))
    y, s = pl.pallas_call(
        _rwkv_chunk_kernel, grid=(batch // nb, nc),
        in_specs=[blk] * 7 + [full] * 3,
        out_specs=[blk, pl.BlockSpec((nb, N_HEADS, DH, DH), lambda b, c: (b, 0, 0, 0))],
        out_shape=[jax.ShapeDtypeStruct((batch, seq_len, W_A), F32),
                   jax.ShapeDtypeStruct((batch, N_HEADS, DH, DH), F32)],
        scratch_shapes=[pltpu.VMEM((nb, N_HEADS // RWKV_GROUP, RWKV_GROUP * DH, RWKV_GROUP * DH), F32)],
        compiler_params=_cparams("parallel", "arbitrary"), name="rwkv_chunk",
    )(*[seq(a) for a in pre], vec(p['gn_g']), vec(p['gn_b']), vec(p['r_k']))
    return y.reshape(batch * seq_len, W_A), s


def _rwkv_step_kernel(r_ref, lw_ref, k_ref, v_ref, kk_ref, a_ref, g_ref, gng_ref, gnb_ref, rk_ref,
                      s_ref, y_ref, so_ref, vt_scr, yt_scr):
    tr = lambda ref: ref[...].T
    r, k, kk, g = tr(r_ref), tr(k_ref), tr(kk_ref), tr(g_ref)
    v = tr(v_ref)
    vt_scr[...] = v
    w = jnp.exp(tr(lw_ref))
    b = kk * tr(a_ref)
    outs = []
    for hh in range(2):
        sl = slice(hh * DH, (hh + 1) * DH)
        rh, kh, nkk, wh, bh = r[sl], k[sl], -kk[sl], w[sl], b[sl]

        def body(i, carry):
            s = s_ref[hh, i]
            sa = jnp.sum(s * nkk, axis=0, keepdims=True)
            s = s * wh + sa * bh + vt_scr[pl.ds(hh * DH + i, 1), :] * kh
            so_ref[hh, i] = s
            yt_scr[pl.ds(hh * DH + i, 1), :] = jnp.sum(s * rh, axis=0, keepdims=True)
            return carry

        lax.fori_loop(0, DH, body, 0)
        yh = yt_scr[sl, :]
        mean = jnp.mean(yh, axis=0, keepdims=True)
        var = jnp.mean(jnp.square(yh - mean), axis=0, keepdims=True)
        yn = (yh - mean) * lax.rsqrt(var + GN_EPS) * gng_ref[sl, :] + gnb_ref[sl, :]
        bonus = jnp.sum(rh * kh * rk_ref[sl, :], axis=0, keepdims=True) * v[sl]
        outs.append((yn + bonus) * g[sl])
    y_ref[...] = jnp.concatenate(outs, axis=0).T


def rwkv_step(pre, state_t, layer, p):
    batch = state_t.shape[-1]
    pair = 2 * DH
    blk = pl.BlockSpec((batch, pair), lambda h: (0, h))
    col = lambda a: a.reshape(W_A, 1)
    cblk = pl.BlockSpec((pair, 1), lambda h: (h, 0))
    y, s = pl.pallas_call(
        _rwkv_step_kernel, grid=(N_HEADS // 2,),
        in_specs=[blk] * 7 + [cblk] * 3 + [pl.BlockSpec((None, 2, DH, DH, batch), lambda h: (layer, h, 0, 0, 0))],
        out_specs=[blk, pl.BlockSpec((2, DH, DH, batch), lambda h: (h, 0, 0, 0))],
        out_shape=[jax.ShapeDtypeStruct((batch, W_A), F32),
                   jax.ShapeDtypeStruct((N_HEADS, DH, DH, batch), F32)],
        scratch_shapes=[pltpu.VMEM((pair, batch), F32), pltpu.VMEM((pair, batch), F32)],
        compiler_params=_cparams("parallel"), name="rwkv_step",
    )(*pre, col(p['gn_g']), col(p['gn_b']), col(p['r_k']), state_t)
    return y, s


def _fox_gate_kernel(f_ref, bf_ref, lf_ref, c_ref, ct_ref):
    t = f_ref.shape[0]
    lf = _log_sigmoid(f_ref[...] + bf_ref[...])
    lf_ref[...] = lf[:, :N_HEADS]
    blk = LANE
    row = _iota2((blk, blk), 0)
    col = _iota2((blk, blk), 1)
    tri = (col <= row).astype(F32)
    carry = jnp.zeros((1, LANE), F32)
    for i in range(t // blk):
        c_blk = _ones_dot_left(tri, lf[i * blk:(i + 1) * blk]) + carry
        c_ref[i * blk:(i + 1) * blk, :] = c_blk
        ct_ref[i] = c_blk.T[:N_HEADS]
        carry = c_blk[blk - 1:blk, :]


def fox_gates(f_pad, b_f, batch, seq_len):
    bf = jnp.zeros((1, LANE), F32).at[0, :N_HEADS].set(b_f)
    return pl.pallas_call(
        _fox_gate_kernel, grid=(batch,),
        in_specs=[pl.BlockSpec((None, seq_len, LANE), lambda b: (b, 0, 0)),
                  pl.BlockSpec((1, LANE), lambda b: (0, 0))],
        out_specs=[pl.BlockSpec((None, seq_len, N_HEADS), lambda b: (b, 0, 0)),
                   pl.BlockSpec((None, seq_len, LANE), lambda b: (b, 0, 0)),
                   pl.BlockSpec((None, seq_len // LANE, N_HEADS, LANE), lambda b: (b, 0, 0, 0))],
        out_shape=[jax.ShapeDtypeStruct((batch, seq_len, N_HEADS), F32),
                   jax.ShapeDtypeStruct((batch, seq_len, LANE), F32),
                   jax.ShapeDtypeStruct((batch, seq_len // LANE, N_HEADS, LANE), F32)],
        compiler_params=_cparams("parallel"), name="fox_gates",
    )(f_pad.reshape(batch, seq_len, LANE), bf)


def _fox_prompt_kernel(q_ref, k_ref, v_ref, c_ref, ct_ref, o_ref, kb_scr, vlo_scr, vhi_scr, *, pair_axis):
    pr = pl.program_id(pair_axis)
    qi = pl.program_id(2)
    tq = q_ref.shape[0]
    tk = tq
    lane = _iota2((1, LANE), 1)
    lo = lane < DH

    @pl.when(qi == 0)
    def _():
        kb_scr[...] = k_ref[...].astype(BF16)
        v = v_ref[...]
        vlo_scr[...] = jnp.where(lo, v, 0.0).astype(BF16)
        vhi_scr[...] = jnp.where(lo, 0.0, v).astype(BF16)

    q = q_ref[...] * QK_SCALE
    q_lo = jnp.where(lo, q, 0.0).astype(BF16)
    q_hi = jnp.where(lo, 0.0, q).astype(BF16)
    c_all = c_ref[...]
    head_lane = _iota2(c_all.shape, 1)
    cq = [jnp.sum(jnp.where(head_lane == 2 * pr + j, c_all, 0.0), axis=-1, keepdims=True)
          for j in range(2)]
    qs = (q_lo, q_hi)
    vs = (vlo_scr, vhi_scr)

    def step(j, carry, masked):
        m, l, acc = carry
        off = pl.multiple_of(j * tk, tk)
        kb = kb_scr[pl.ds(off, tk), :]
        two = range(2)
        ck = [jnp.concatenate([ct_ref[j * (tk // LANE) + i, pl.ds(2 * pr + hh, 1), :]
                               for i in range(tk // LANE)], axis=-1) for hh in two]
        s = [_bdot_nt(qs[hh], kb) + cq[hh] - ck[hh] for hh in two]
        if masked:
            causal = _iota2(s[0].shape, 1) <= _iota2(s[0].shape, 0)
            s = [jnp.where(causal, s[hh], NEG_INF) for hh in two]
        mn = [jnp.maximum(m[hh], jnp.max(s[hh], axis=-1, keepdims=True)) for hh in two]
        alphas = [jnp.exp(m[hh] - mn[hh]) for hh in two]
        pmat = [jnp.exp(s[hh] - mn[hh]) for hh in two]
        pvs = [jnp.dot(pmat[hh].astype(BF16), vs[hh][pl.ds(off, tk), :], preferred_element_type=F32)
               for hh in two]
        new_l = [l[hh] * alphas[hh] + jnp.sum(pmat[hh], axis=-1, keepdims=True) for hh in two]
        acc = acc * jnp.where(lo, alphas[0], alphas[1]) + pvs[0] + pvs[1]
        return tuple(mn), tuple(new_l), acc

    init = ((jnp.full((tq, 1), NEG_INF, F32),) * 2, (jnp.zeros((tq, 1), F32),) * 2,
            jnp.zeros((tq, LANE), F32))
    carry = lax.fori_loop(0, qi, lambda j, cr: step(j, cr, False), init)
    m, l, acc = step(qi, carry, True)
    o_ref[...] = acc / jnp.where(lo, l[0], l[1])


def fox_prompt(q, k, v, c, ct, batch, seq_len):
    tq = min(ATT_BLOCK, seq_len)
    seq = lambda a: a.reshape(batch, seq_len, W_A)
    kv_spec = pl.BlockSpec((None, seq_len, LANE), lambda b, pr, i: (b, 0, pr))
    o = pl.pallas_call(
        functools.partial(_fox_prompt_kernel, pair_axis=1),
        grid=(batch, N_HEADS // 2, seq_len // tq),
        in_specs=[pl.BlockSpec((None, tq, LANE), lambda b, pr, i: (b, i, pr)), kv_spec, kv_spec,
                  pl.BlockSpec((None, tq, LANE), lambda b, pr, i: (b, i, 0)),
                  pl.BlockSpec((None, seq_len // LANE, N_HEADS, LANE), lambda b, pr, i: (b, 0, 0, 0))],
        out_specs=pl.BlockSpec((None, tq, LANE), lambda b, pr, i: (b, i, pr)),
        out_shape=jax.ShapeDtypeStruct((batch, seq_len, W_A), F32),
        scratch_shapes=[pltpu.VMEM((seq_len, LANE), BF16)] * 3,
        compiler_params=_cparams("parallel", "parallel", "arbitrary"), name="fox_prompt",
    )(seq(q), seq(k), seq(v), c, ct)
    return o.reshape(batch * seq_len, W_A)


FOX_PAGES_PER_STEP = 8


def _fox_decode_kernel(pt_ref, q_ref, kn_ref, vn_ref, lfn_ref, *refs, n_pages):
    g = n_pages
    lf_refs = refs[0:g]
    k_refs = refs[g:2 * g]
    v_refs = refs[2 * g:3 * g]
    o_ref = refs[3 * g]
    qb_scr, m_scr, l_scr, acc_scr, tail_scr = refs[3 * g + 1:]
    step = pl.program_id(1)
    page = k_refs[0].shape[-1]
    eye = _iota2((DH, DH), 0) == _iota2((DH, DH), 1)

    @pl.when(step == 0)
    def _():
        m_scr[...] = jnp.full_like(m_scr, NEG_INF)
        l_scr[...] = jnp.zeros_like(l_scr)
        acc_scr[...] = jnp.zeros_like(acc_scr)
        tail_scr[...] = jnp.zeros_like(tail_scr)
        q = q_ref[...] * QK_SCALE
        for h in range(N_HEADS):
            q_col = jnp.sum(jnp.where(eye, q[:, h * DH:(h + 1) * DH], 0.0), axis=-1, keepdims=True)
            qb_scr[h] = jnp.broadcast_to(q_col, (DH, page))

    lfn = lfn_ref[...]
    after = (_iota2((page, page), 0) > _iota2((page, page), 1)).astype(F32)
    tail = tail_scr[...]
    decs = []
    for i in range(g):
        lf = lf_refs[i][...]
        decs.append(_ones_dot_right(lf, after) + tail + lfn)
        tail = tail + jnp.sum(lf, axis=-1, keepdims=True)
    tail_scr[...] = tail
    dec = jnp.concatenate(decs, axis=-1)
    for h in range(N_HEADS):
        qb = qb_scr[h]
        s = jnp.concatenate([jnp.sum(qb * k_refs[i][h], axis=0, keepdims=True) for i in range(g)],
                            axis=-1) + dec[h:h + 1, :]
        m_old = m_scr[h][:, :1]
        mn = jnp.maximum(m_old, jnp.max(s, axis=-1, keepdims=True))
        alpha = jnp.exp(m_old - mn)
        pmat = jnp.exp(s - mn)
        l_new = l_scr[h][:, :1] * alpha + jnp.sum(pmat, axis=-1, keepdims=True)
        m_scr[h] = jnp.broadcast_to(mn, (1, LANE))
        l_scr[h] = jnp.broadcast_to(l_new, (1, LANE))
        acc = acc_scr[h] * alpha
        for i in range(g):
            acc = acc + pmat[:, i * page:(i + 1) * page] * v_refs[i][h]
        acc_scr[h] = acc

    @pl.when(step == pl.num_programs(1) - 1)
    def _():
        q = q_ref[...] * QK_SCALE
        kn = kn_ref[...]
        vn = vn_ref[...]
        outs = []
        for h in range(N_HEADS):
            sl = slice(h * DH, (h + 1) * DH)
            s_new = jnp.sum(q[:, sl] * kn[:, sl], axis=-1, keepdims=True)
            m_old = m_scr[h][:, :1]
            mn = jnp.maximum(m_old, s_new)
            alpha = jnp.exp(m_old - mn)
            p_new = jnp.exp(s_new - mn)
            lt = l_scr[h][:, :1] * alpha + p_new
            o_col = jnp.sum(acc_scr[h], axis=-1, keepdims=True) * alpha
            o_row = jnp.sum(jnp.where(eye, o_col, 0.0), axis=0, keepdims=True)
            outs.append((o_row + p_new * vn[:, sl]) / lt)
        o_ref[...] = jnp.concatenate(outs, axis=-1)


def fox_decode(q, k_new, v_new, logf_new, cache_kt, cache_vt, cache_lf_t, layer, page_table):
    batch, n_pages = page_table.shape
    page = cache_kt.shape[-1]
    g = min(FOX_PAGES_PER_STEP, n_pages)
    steps = n_pages // g
    tok = lambda a: a.reshape(batch, 1, W_A)
    tok_spec = pl.BlockSpec((None, 1, W_A), lambda b, s, pt: (b, 0, 0))

    def page_idx(b, s, pt, i):
        return pt[b * n_pages + (n_pages - 1 - (s * g + i))]

    lf_specs = [pl.BlockSpec((None, None, N_HEADS, page),
                             lambda b, s, pt, i=i: (layer, page_idx(b, s, pt, i), 0, 0)) for i in range(g)]
    kv_specs = [pl.BlockSpec((None, None, N_HEADS, DH, page),
                             lambda b, s, pt, i=i: (layer, page_idx(b, s, pt, i), 0, 0, 0)) for i in range(g)]
    grid_spec = pltpu.PrefetchScalarGridSpec(
        num_scalar_prefetch=1, grid=(batch, steps),
        in_specs=[tok_spec, tok_spec, tok_spec,
                  pl.BlockSpec((None, N_HEADS, 1), lambda b, s, pt: (b, 0, 0))] + lf_specs + kv_specs + kv_specs,
        out_specs=tok_spec,
        scratch_shapes=[pltpu.VMEM((N_HEADS, DH, page), F32), pltpu.VMEM((N_HEADS, 1, LANE), F32),
                        pltpu.VMEM((N_HEADS, 1, LANE), F32), pltpu.VMEM((N_HEADS, DH, page), F32),
                        pltpu.VMEM((N_HEADS, 1), F32)])
    o = pl.pallas_call(
        functools.partial(_fox_decode_kernel, n_pages=g), grid_spec=grid_spec,
        out_shape=jax.ShapeDtypeStruct((batch, 1, W_A), F32),
        compiler_params=_cparams("parallel", "arbitrary"), name="fox_decode",
    )(page_table.reshape(-1), tok(q), tok(k_new), tok(v_new), logf_new.reshape(batch, N_HEADS, 1),
      *([cache_lf_t] * g), *([cache_kt] * g), *([cache_vt] * g))
    return o.reshape(batch, W_A)


def _t5_buckets(dist):
    n = np.maximum(dist, 0)
    max_exact = N_BUCKETS // 2
    nf = np.maximum(n, 1).astype(np.float32)
    large = max_exact + (np.log(nf / max_exact) / math.log(MAX_DISTANCE / max_exact)
                         * (N_BUCKETS - max_exact)).astype(np.int32)
    large = np.minimum(large, N_BUCKETS - 1)
    return np.where(n < max_exact, n, large).astype(np.int32)


def _bias_gather_kernel(idx_ref, table_ref, o_ref):
    col = pl.program_id(0)
    idx = idx_ref[...]
    out = jnp.zeros(idx.shape, F32)
    for bkt in range(N_BUCKETS):
        out = jnp.where(idx == bkt, table_ref[bkt, col], out)
    o_ref[...] = out


def bias_gather(idx, table):
    r, c = idx.shape
    ncol = table.shape[1]
    return pl.pallas_call(
        _bias_gather_kernel, grid=(ncol,),
        in_specs=[pl.BlockSpec((r, c), lambda j: (0, 0)),
                  pl.BlockSpec(memory_space=pltpu.SMEM)],
        out_specs=pl.BlockSpec((None, r, c), lambda j: (j, 0, 0)),
        out_shape=jax.ShapeDtypeStruct((ncol, r, c), F32),
        compiler_params=_cparams("parallel"), name="bias_gather")(idx, table)


def _diff_lambda(lq1, lk1, lq2, lk2, lambda_init):
    return (jnp.exp(jnp.sum(lq1 * lk1, axis=-1, keepdims=True))
            - jnp.exp(jnp.sum(lq2 * lk2, axis=-1, keepdims=True)) + lambda_init)


def _diff_prompt_kernel(q_ref, k_ref, v_ref, bias_ref, far_ref, lam_ref, sub_ref, o_ref, kb_scr, vb_scr,
                        *, lambda_init):
    qi = pl.program_id(2)
    tq = q_ref.shape[0]
    tk = tq
    lane = _iota2((1, LANE), 1)
    lo = lane < DH

    @pl.when(qi == 0)
    def _():
        kb_scr[...] = k_ref[...].astype(BF16)
        vb_scr[...] = v_ref[...].astype(BF16)

    q = q_ref[...] * QK_SCALE
    qs = (jnp.where(lo, q, 0.0).astype(BF16), jnp.where(lo, 0.0, q).astype(BF16))
    head = pl.program_id(1)
    fars = (far_ref[head, 0], far_ref[head, 1])

    def step(j, carry, kind):
        m, l, acc = carry
        off = pl.multiple_of(j * tk, tk)
        kb = kb_scr[pl.ds(off, tk), :]
        vb = vb_scr[pl.ds(off, tk), :]
        two = range(2)
        s = [_bdot_nt(qs[mp], kb) for mp in two]
        if kind == 'far':
            s = [s[mp] + fars[mp] for mp in two]
        elif kind == 'near':
            s = [s[mp] + bias_ref[mp, 1] for mp in two]
        else:
            causal = _iota2(s[0].shape, 1) <= _iota2(s[0].shape, 0)
            s = [jnp.where(causal, s[mp] + bias_ref[mp, 0], NEG_INF) for mp in two]
        mn = [jnp.maximum(m[mp], jnp.max(s[mp], axis=-1, keepdims=True)) for mp in two]
        alpha = [jnp.exp(m[mp] - mn[mp]) for mp in two]
        pmat = [jnp.exp(s[mp] - mn[mp]) for mp in two]
        pv = [jnp.dot(pmat[mp].astype(BF16), vb, preferred_element_type=F32) for mp in two]
        new_l = [l[mp] * alpha[mp] + jnp.sum(pmat[mp], axis=-1, keepdims=True) for mp in two]
        new_acc = [acc[mp] * alpha[mp] + pv[mp] for mp in two]
        return tuple(mn), tuple(new_l), tuple(new_acc)

    init = ((jnp.full((tq, 1), NEG_INF, F32),) * 2, (jnp.zeros((tq, 1), F32),) * 2,
            (jnp.zeros((tq, LANE), F32),) * 2)
    carry = lax.fori_loop(0, jnp.maximum(qi - 1, 0), lambda j, cr: step(j, cr, 'far'), init)
    carry = lax.cond(qi >= 1, lambda cr: step(qi - 1, cr, 'near'), lambda cr: cr, carry)
    m, l, acc = step(qi, carry, 'diag')
    lam = _diff_lambda(lam_ref[0:1, :], lam_ref[1:2, :], lam_ref[2:3, :], lam_ref[3:4, :], lambda_init)
    o = acc[0] / l[0] - lam * (acc[1] / l[1])
    o_ref[...] = _rmsnorm(o, sub_ref[...]) * (1.0 - lambda_init)


def diff_prompt(q, k, v, bias_tiles, far_bias, lam_vecs, subln_g, lambda_init, batch, seq_len):
    tq = min(ATT_BLOCK, seq_len)
    seq = lambda a: a.reshape(batch, seq_len, W_C)
    kv_spec = pl.BlockSpec((None, seq_len, LANE), lambda b, h, i: (b, 0, h))
    blk = pl.BlockSpec((None, tq, LANE), lambda b, h, i: (b, i, h))
    o = pl.pallas_call(
        functools.partial(_diff_prompt_kernel, lambda_init=lambda_init),
        grid=(batch, N_HEADS, seq_len // tq),
        in_specs=[blk, kv_spec, kv_spec,
                  pl.BlockSpec((None, 2, 2, tq, tq), lambda b, h, i: (h, 0, 0, 0, 0)),
                  pl.BlockSpec(memory_space=pltpu.SMEM),
                  pl.BlockSpec((4, DH), lambda b, h, i: (0, 0)),
                  pl.BlockSpec((1, LANE), lambda b, h, i: (0, 0))],
        out_specs=blk,
        out_shape=jax.ShapeDtypeStruct((batch, seq_len, W_C), F32),
        scratch_shapes=[pltpu.VMEM((seq_len, LANE), BF16)] * 2,
        compiler_params=_cparams("parallel", "parallel", "arbitrary"), name="diff_prompt",
    )(seq(q), seq(k), seq(v), bias_tiles, far_bias, lam_vecs, subln_g.reshape(1, LANE))
    return o.reshape(batch * seq_len, W_C)


DIFF_PAGES_PER_STEP = 8


def _diff_decode_kernel(pt_ref, q_ref, kn_ref, vn_ref, bias_ref, rel_ref, lam_ref, sub_ref, *refs,
                        n_pages, lambda_init):
    g = n_pages
    k_refs = refs[0:g]
    v_refs = refs[g:2 * g]
    o_ref = refs[2 * g]
    m_scr, l_scr, acc_scr = refs[2 * g + 1:]
    step = pl.program_id(1)

    @pl.when(step == 0)
    def _():
        m_scr[...] = jnp.full_like(m_scr, NEG_INF)
        l_scr[...] = jnp.zeros_like(l_scr)
        acc_scr[...] = jnp.zeros_like(acc_scr)

    row = _iota2((N_HEADS, LANE), 0)
    lane = _iota2((N_HEADS, LANE), 1)
    qsel = ((row == 0) & (lane < DH)) | ((row == 1) & (lane >= DH))

    def q_tile(h):
        return jnp.where(qsel, q_ref[h:h + 1, :] * QK_SCALE, 0.0)

    page = k_refs[0].shape[0] // N_HEADS
    heads = range(N_HEADS)

    def head_rows(refs_, h):
        return jnp.concatenate([r_[pl.ds(h, page, stride=N_HEADS), :] for r_ in refs_], axis=0)

    s = [lax.dot_general(q_tile(h), head_rows(k_refs, h), (((1,), (1,)), ((), ())),
                         preferred_element_type=F32) + bias_ref[h] for h in heads]
    m_old = [m_scr[h][:, :1] for h in heads]
    mn = [jnp.maximum(m_old[h], jnp.max(s[h], axis=-1, keepdims=True)) for h in heads]
    alpha = [jnp.exp(m_old[h] - mn[h]) for h in heads]
    pmat = [jnp.exp(s[h] - mn[h]) for h in heads]
    pv = [jnp.dot(pmat[h], head_rows(v_refs, h), preferred_element_type=F32) for h in heads]
    for h in heads:
        l_new = l_scr[h][:, :1] * alpha[h] + jnp.sum(pmat[h], axis=-1, keepdims=True)
        m_scr[h] = jnp.broadcast_to(mn[h], (N_HEADS, LANE))
        l_scr[h] = jnp.broadcast_to(l_new, (N_HEADS, LANE))
        acc_scr[h] = acc_scr[h] * alpha[h] + pv[h]

    @pl.when(step == pl.num_programs(1) - 1)
    def _():
        lam = _diff_lambda(lam_ref[0:1, :], lam_ref[1:2, :], lam_ref[2:3, :], lam_ref[3:4, :], lambda_init)
        for h in range(N_HEADS):
            b_new = jnp.where(row[:, :1] == 0, rel_ref[0, 2 * h], rel_ref[0, 2 * h + 1])
            s_new = jnp.sum(q_tile(h) * kn_ref[h:h + 1, :], axis=-1, keepdims=True) + b_new
            m_old = m_scr[h][:, :1]
            mn = jnp.maximum(m_old, s_new)
            alpha = jnp.exp(m_old - mn)
            p_new = jnp.exp(s_new - mn)
            lt = l_scr[h][:, :1] * alpha + p_new
            o = (acc_scr[h] * alpha + p_new * vn_ref[h:h + 1, :]) / lt
            od = o[0:1] - lam * o[1:2]
            od = od * lax.rsqrt(jnp.mean(od * od, axis=-1, keepdims=True) + RMS_EPS)
            o_ref[:, h * LANE:(h + 1) * LANE] = od * sub_ref[...] * (1.0 - lambda_init)


def diff_decode(q, k_new, v_new, cache_k, cache_v, layer, page_table, page_bias, rel_bias, lam_vecs,
                subln_g, lambda_init):
    batch, n_pages = page_table.shape
    page = cache_k.shape[2]
    g = min(DIFF_PAGES_PER_STEP, n_pages)
    steps = n_pages // g
    heads = lambda a: a.reshape(batch, N_HEADS, LANE)
    head_spec = pl.BlockSpec((None, N_HEADS, LANE), lambda b, s, pt: (b, 0, 0))
    pool = cache_k.shape[1]
    cache_k = cache_k.reshape(-1, pool, page * N_HEADS, LANE)
    cache_v = cache_v.reshape(-1, pool, page * N_HEADS, LANE)
    kv_specs = [pl.BlockSpec((None, None, page * N_HEADS, LANE),
                             lambda b, s, pt, i=i: (layer, pt[b * n_pages + s * g + i], 0, 0))
                for i in range(g)]
    grid_spec = pltpu.PrefetchScalarGridSpec(
        num_scalar_prefetch=1, grid=(batch, steps),
        in_specs=[head_spec, head_spec, head_spec,
                  pl.BlockSpec((None, N_HEADS, N_HEADS, g * page), lambda b, s, pt: (s, 0, 0, 0)),
                  pl.BlockSpec(memory_space=pltpu.SMEM),
                  pl.BlockSpec((4, DH), lambda b, s, pt: (0, 0)),
                  pl.BlockSpec((1, LANE), lambda b, s, pt: (0, 0))] + kv_specs + kv_specs,
        out_specs=pl.BlockSpec((None, 1, W_C), lambda b, s, pt: (b, 0, 0)),
        scratch_shapes=[pltpu.VMEM((N_HEADS, N_HEADS, LANE), F32)] * 3)
    o = pl.pallas_call(
        functools.partial(_diff_decode_kernel, n_pages=g, lambda_init=lambda_init), grid_spec=grid_spec,
        out_shape=jax.ShapeDtypeStruct((batch, 1, W_C), F32),
        compiler_params=_cparams("parallel", "arbitrary"), name="diff_decode",
    )(page_table.reshape(-1), heads(q), heads(k_new), heads(v_new), page_bias, rel_bias, lam_vecs,
      subln_g.reshape(1, LANE), *([cache_k] * g), *([cache_v] * g))
    return o.reshape(batch, W_C)


def _prep_params(norm_mix, norm_ffn, norm_final, w_in_ab, w_out_ab, rwkv_mu, rwkv_w0, rwkv_w_up, rwkv_a0,
                 rwkv_a_up, rwkv_g_up, rwkv_k_k, rwkv_k_a, rwkv_r_k, rwkv_gn_g, rwkv_gn_b, fox_b_f, w_in_c,
                 w_out_c, diff_lq1, diff_lk1, diff_lq2, diff_lk2, diff_subln, rel_bias, ffn_gate, ffn_up,
                 ffn_down):
    depth = norm_mix.shape[0]
    o_r, o_wd, o_k, o_v, o_ad, o_gd = np.cumsum([0, W_A, LORA_W, W_A, W_A, LORA_A]).tolist()
    perm = np.concatenate([np.arange(o_r, o_r + W_A), np.arange(o_k, o_k + W_A), np.arange(o_v, o_v + W_A),
                           np.arange(o_wd, o_wd + LORA_W), np.arange(o_ad, o_ad + LORA_A),
                           np.arange(o_gd, o_gd + LORA_G)])
    layers = []
    for li in range(depth):
        j = li // 2
        lp = {'norm_mix': norm_mix[li], 'norm_ffn': norm_ffn[li],
              'ffn_gate': ffn_gate[li].astype(BF16), 'ffn_up': ffn_up[li].astype(BF16),
              'ffn_down': ffn_down[li].astype(BF16)}
        if li % 2 == 0:
            w = w_in_ab[j]
            wa = w[:, :A_COLS][:, perm]
            wb = w[:, A_COLS:]
            wf = jnp.pad(wb[:, 3 * W_A:], ((0, 0), (0, LANE - N_HEADS)))
            lp['w_in'] = [wa.astype(BF16), wb[:, :W_A].astype(BF16), wb[:, W_A:2 * W_A].astype(BF16),
                          wb[:, 2 * W_A:3 * W_A].astype(BF16), wf.astype(BF16)]
            lp['w_out'] = [w_out_ab[j][:W_A].astype(BF16), w_out_ab[j][W_A:].astype(BF16)]
            zeros = jnp.zeros((LORA_W, W_A), F32)
            w_lora = jnp.concatenate([jnp.concatenate([rwkv_w_up[j], zeros], axis=1),
                                      jnp.concatenate([zeros, rwkv_a_up[j]], axis=1)], axis=0)
            lp['rwkv'] = {'mu': rwkv_mu[j][perm], 'w_lora': w_lora.astype(BF16),
                          'g_up': rwkv_g_up[j].astype(BF16), 'w0': rwkv_w0[j], 'a0': rwkv_a0[j],
                          'k_k': rwkv_k_k[j], 'k_a': rwkv_k_a[j], 'r_k': rwkv_r_k[j].reshape(-1),
                          'gn_g': rwkv_gn_g[j], 'gn_b': rwkv_gn_b[j]}
            lp['fox_b_f'] = fox_b_f[j]
        else:
            w = w_in_c[j]
            lp['w_in'] = [w[:, :W_C].astype(BF16), w[:, W_C:2 * W_C].astype(BF16), w[:, 2 * W_C:].astype(BF16)]
            lp['w_out'] = [w_out_c[j].astype(BF16)]
            lp['lam_vecs'] = jnp.stack([diff_lq1[j], diff_lk1[j], diff_lq2[j], diff_lk2[j]])
            lp['subln'] = diff_subln[j]
            lp['lambda_init'] = 0.8 - 0.6 * math.exp(-0.3 * li)
        layers.append(lp)
    return layers, norm_final, rel_bias


def _prompt_bias(rel_bias, seq_len):
    tq = min(ATT_BLOCK, seq_len)
    d = np.arange(tq)[:, None] - np.arange(tq)[None, :]
    idx = np.concatenate([_t5_buckets(d), _t5_buckets(d + tq)], axis=0)
    tiles = bias_gather(jnp.asarray(idx), rel_bias)
    tiles = tiles.reshape(N_HEADS, 2, 2, tq, tq)
    far_idx = np.full((8, LANE), int(_t5_buckets(np.array([2 * tq]))[0]), np.int32)
    far = bias_gather(jnp.asarray(far_idx), rel_bias)[:, 0, 0].reshape(N_HEADS, 2)
    return tiles, far


def _decode_bias(rel_bias, n_pages, page, g):
    t0 = n_pages * page
    steps = n_pages // g
    rows = -(-steps // 8) * 8
    idx = np.zeros((rows, g * page), np.int32)
    idx[:steps] = _t5_buckets(t0 - np.arange(t0).reshape(steps, g * page))
    bias = bias_gather(jnp.asarray(idx), rel_bias)[:, :steps]
    bias = jnp.transpose(bias.reshape(N_HEADS, 2, steps, g * page), (2, 0, 1, 3))
    return jnp.pad(bias, ((0, 0), (0, 0), (0, N_HEADS - 2), (0, 0)))


def _decode_logf_kernel(f_ref, bf_ref, o_ref):
    o_ref[...] = _log_sigmoid(f_ref[...] + bf_ref[...])[:, :N_HEADS]


def _decode_logf(f_pad, b_f):
    bf = jnp.zeros((1, LANE), F32).at[0, :N_HEADS].set(b_f)
    return pl.pallas_call(
        _decode_logf_kernel, out_shape=jax.ShapeDtypeStruct((f_pad.shape[0], N_HEADS), F32),
        name="decode_logf")(f_pad, bf)


def _trunk(x, layers, norm_final, batch, seq_len, past, prompt_bias=None):
    fk, fv, fl, dk, dv, rs, sh = [], [], [], [], [], [], []
    y = None
    for li, lp in enumerate(layers):
        j = li // 2
        if li % 2 == 0:
            pa, q, k, v, f_pad, xn = norm_proj(x, lp['norm_mix'], lp['w_in'], emit_xn=True)
            if past is None:
                pre = rwkv_pre(pa, None, seq_len, lp['rwkv'])
                ya, s_fin = rwkv_chunked(pre, batch, seq_len, lp['rwkv'])
                logf, c, ct = fox_gates(f_pad, lp['fox_b_f'], batch, seq_len)
                yb = fox_prompt(q, k, v, c, ct, batch, seq_len)
            else:
                (pa_prev,) = norm_proj(past['state_shift'][j], None, lp['w_in'][:1])
                pre = rwkv_pre(pa, pa_prev, seq_len, lp['rwkv'])
                ya, s_fin = rwkv_step(pre, past['state_rwkv_t'], j, lp['rwkv'])
                logf = _decode_logf(f_pad, lp['fox_b_f'])
                yb = fox_decode(q, k, v, logf, past['cache_fox_kt'], past['cache_fox_vt'],
                                past['cache_fox_logf_t'], j, past['page_table'])
                logf = logf.reshape(batch, 1, N_HEADS)
            x = proj_residual(x, [ya, yb], lp['w_out'])
            fk.append(k.reshape(batch, seq_len, N_HEADS, DH))
            fv.append(v.reshape(batch, seq_len, N_HEADS, DH))
            fl.append(logf)
            rs.append(s_fin)
            sh.append(xn.reshape(batch, seq_len, D_MODEL)[:, -1])
        else:
            q, k, v = norm_proj(x, lp['norm_mix'], lp['w_in'])
            if past is None:
                yc = diff_prompt(q, k, v, prompt_bias[0], prompt_bias[1], lp['lam_vecs'],
                                 lp['subln'], lp['lambda_init'], batch, seq_len)
            else:
                yc = diff_decode(q, k, v, past['cache_diff_k'], past['cache_diff_v'], j, past['page_table'],
                                 past['page_bias'], past['rel_bias'], lp['lam_vecs'], lp['subln'],
                                 lp['lambda_init'])
            x = proj_residual(x, [yc], lp['w_out'])
            dk.append(k.reshape(batch, seq_len, N_HEADS, 2 * DH))
            dv.append(v.reshape(batch, seq_len, N_HEADS, 2 * DH))
        if li == len(layers) - 1:
            x, y = ffn_residual(x, lp['norm_ffn'], lp['ffn_gate'], lp['ffn_up'], lp['ffn_down'], norm_final)
        else:
            x = ffn_residual(x, lp['norm_ffn'], lp['ffn_gate'], lp['ffn_up'], lp['ffn_down'])
    rs = jnp.stack(rs)
    if past is not None:
        rs = jnp.transpose(rs, (0, 4, 1, 2, 3))
    return (y.reshape(batch, seq_len, D_MODEL), jnp.stack(fk), jnp.stack(fv), jnp.stack(fl), jnp.stack(dk),
            jnp.stack(dv), rs, jnp.stack(sh))


def kernel(x_prompt, x_sample, cache_fox_k, cache_fox_v, cache_fox_logf, cache_diff_k, cache_diff_v, state_rwkv, state_shift, page_table, norm_mix, norm_ffn, norm_final, w_in_ab, w_out_ab, rwkv_mu, rwkv_w0, rwkv_w_up, rwkv_a0, rwkv_a_up, rwkv_g_up, rwkv_k_k, rwkv_k_a, rwkv_r_k, rwkv_gn_g, rwkv_gn_b, fox_b_f, w_in_c, w_out_c, diff_lq1, diff_lk1, diff_lq2, diff_lk2, diff_subln, rel_bias, ffn_gate, ffn_up, ffn_down):
    layers, norm_final, rel_bias = _prep_params(
        norm_mix, norm_ffn, norm_final, w_in_ab, w_out_ab, rwkv_mu, rwkv_w0, rwkv_w_up, rwkv_a0, rwkv_a_up,
        rwkv_g_up, rwkv_k_k, rwkv_k_a, rwkv_r_k, rwkv_gn_g, rwkv_gn_b, fox_b_f, w_in_c, w_out_c, diff_lq1,
        diff_lk1, diff_lq2, diff_lk2, diff_subln, rel_bias, ffn_gate, ffn_up, ffn_down)
    bp, tp, d = x_prompt.shape
    bs, ts, _ = x_sample.shape
    n_pages = page_table.shape[1]
    page = cache_fox_k.shape[2]
    past = {
        'state_shift': state_shift, 'page_table': page_table,
        'state_rwkv_t': jnp.transpose(state_rwkv, (0, 2, 3, 4, 1)),
        'cache_fox_kt': jnp.transpose(cache_fox_k, (0, 1, 3, 4, 2)),
        'cache_fox_vt': jnp.transpose(cache_fox_v, (0, 1, 3, 4, 2)),
        'cache_fox_logf_t': jnp.swapaxes(cache_fox_logf, 2, 3),
        'cache_diff_k': cache_diff_k, 'cache_diff_v': cache_diff_v, 'rel_bias': rel_bias,
        'page_bias': _decode_bias(rel_bias, n_pages, page, min(DIFF_PAGES_PER_STEP, n_pages)),
    }
    out_p = _trunk(x_prompt.reshape(bp * tp, d), layers, norm_final, bp, tp, None,
                   _prompt_bias(rel_bias, tp))
    out_s = _trunk(x_sample.reshape(bs * ts, d), layers, norm_final, bs, ts, past)
    return (out_p[0], out_s[0]) + out_p[1:] + out_s[1:]
```

```python
import functools
import math

import jax
import jax.numpy as jnp
import numpy as np
from jax import lax
from jax.experimental import pallas as pl
from jax.experimental.pallas import tpu as pltpu

F32 = jnp.float32
BF16 = jnp.bfloat16

D_MODEL = 1024
N_HEADS = 8
DH = 64
W_A = N_HEADS * DH
LORA_W = 64
LORA_A = 64
LORA_G = 128
A_COLS = 3 * W_A + LORA_W + LORA_A + LORA_G
W_C = N_HEADS * 2 * DH
N_BUCKETS = 32
MAX_DISTANCE = 128
RMS_EPS = 1e-6
GN_EPS = 64e-5
NEG_INF = -1e30
QK_SCALE = DH ** -0.5

LANE = 128
RWKV_CHUNK = 64
RWKV_GROUP = 4
RWKV_SEQS = 4
ATT_BLOCK = 512
VMEM_LIMIT = 48 * 1024 * 1024


def _cparams(*sem):
    return pltpu.CompilerParams(dimension_semantics=sem, vmem_limit_bytes=VMEM_LIMIT)


def _bdot(a, b):
    return jnp.dot(a.astype(BF16), b.astype(BF16), preferred_element_type=F32)


def _bdot_nt(a, b):
    return lax.dot_general(a.astype(BF16), b.astype(BF16), (((1,), (1,)), ((), ())),
                           preferred_element_type=F32)


def _bdot_tn(a, b):
    return lax.dot_general(a.astype(BF16), b.astype(BF16), (((0,), (0,)), ((), ())),
                           preferred_element_type=F32)


def _split3(x):
    hi = x.astype(BF16).astype(F32)
    r1 = x - hi
    mid = r1.astype(BF16).astype(F32)
    lo = (r1 - mid).astype(BF16).astype(F32)
    return hi, mid, lo


def _ones_dot_left(ones_mat, x):
    n = x.shape[1]
    stack = jnp.concatenate(_split3(x), axis=1).astype(BF16)
    out = jnp.dot(ones_mat.astype(BF16), stack, preferred_element_type=F32)
    return out[:, :n] + out[:, n:2 * n] + out[:, 2 * n:]


def _ones_dot_right(x, ones_mat):
    m = x.shape[0]
    stack = jnp.concatenate(_split3(x), axis=0).astype(BF16)
    out = jnp.dot(stack, ones_mat.astype(BF16), preferred_element_type=F32)
    return out[:m] + out[m:2 * m] + out[2 * m:]


def _sigmoid(x):
    return 1.0 / (1.0 + jnp.exp(-x))


def _log_sigmoid(x):
    return jnp.minimum(x, 0.0) - jnp.log(1.0 + jnp.exp(-jnp.abs(x)))


def _softplus(x):
    return jnp.maximum(x, 0.0) + jnp.log(1.0 + jnp.exp(-jnp.abs(x)))


def _rmsnorm(x, g):
    return x * lax.rsqrt(jnp.mean(x * x, axis=-1, keepdims=True) + RMS_EPS) * g


def _iota2(shape, dim):
    return lax.broadcasted_iota(jnp.int32, shape, dim)


def _row_tile(m):
    for t in (512, 256, 128):
        if m % t == 0:
            return t
    return m


def _norm_proj_kernel(*refs, n_out, with_norm, emit_xn, stacked):
    x_ref = refs[0]
    pos = 1
    if with_norm:
        g_ref = refs[pos]
        pos += 1
    w_refs = refs[pos:pos + n_out]
    pos += n_out
    prev_refs = dict(zip(stacked, refs[pos:pos + len(stacked)]))
    pos += len(stacked)
    out_refs = refs[pos:pos + n_out]
    pos += n_out
    x = x_ref[...]
    if with_norm:
        x = _rmsnorm(x, g_ref[...])
        if emit_xn:
            refs[pos][...] = x
    xb = x.astype(BF16)
    for idx, (w_ref, o_ref) in enumerate(zip(w_refs, out_refs)):
        res = jnp.dot(xb, w_ref[...], preferred_element_type=F32)
        if idx in prev_refs:
            o_ref[0] = prev_refs[idx][...]
            o_ref[1] = res
        else:
            o_ref[...] = res


def norm_proj(x, g, ws, emit_xn=False, stack_prev=None):
    m, d = x.shape
    tm = _row_tile(m)
    with_norm = g is not None
    stack_prev = stack_prev or {}
    stacked = tuple(sorted(stack_prev))
    ins = [x]
    in_specs = [pl.BlockSpec((tm, d), lambda i: (i, 0))]
    if with_norm:
        ins.append(g.reshape(1, d))
        in_specs.append(pl.BlockSpec((1, d), lambda i: (0, 0)))
    for w in ws:
        ins.append(w)
        in_specs.append(pl.BlockSpec(w.shape, lambda i: (0, 0), pipeline_mode=pl.Buffered(1)))
    for idx in stacked:
        ins.append(stack_prev[idx])
        in_specs.append(pl.BlockSpec((tm, ws[idx].shape[1]), lambda i: (i, 0)))
    out_shape, out_specs = [], []
    for idx, w in enumerate(ws):
        n = w.shape[1]
        if idx in stack_prev:
            out_shape.append(jax.ShapeDtypeStruct((2, m, n), F32))
            out_specs.append(pl.BlockSpec((2, tm, n), lambda i: (0, i, 0)))
        else:
            out_shape.append(jax.ShapeDtypeStruct((m, n), F32))
            out_specs.append(pl.BlockSpec((tm, n), lambda i: (i, 0)))
    if emit_xn:
        out_shape.append(jax.ShapeDtypeStruct((m, d), F32))
        out_specs.append(pl.BlockSpec((tm, d), lambda i: (i, 0)))
    return pl.pallas_call(
        functools.partial(_norm_proj_kernel, n_out=len(ws), with_norm=with_norm, emit_xn=emit_xn,
                          stacked=stacked),
        grid=(m // tm,), in_specs=in_specs, out_specs=out_specs, out_shape=out_shape,
        compiler_params=_cparams("parallel"), name="norm_proj")(*ins)


def _proj_residual_kernel(*refs, n_in):
    x_ref = refs[0]
    y_refs = refs[1:1 + n_in]
    w_refs = refs[1 + n_in:1 + 2 * n_in]
    o_ref = refs[1 + 2 * n_in]
    acc = x_ref[...]
    for y_ref, w_ref in zip(y_refs, w_refs):
        acc = acc + jnp.dot(y_ref[...].astype(BF16), w_ref[...], preferred_element_type=F32)
    o_ref[...] = acc


def proj_residual(x, ys, ws):
    m, d = x.shape
    tm = _row_tile(m)
    in_specs = [pl.BlockSpec((tm, d), lambda i: (i, 0))]
    in_specs += [pl.BlockSpec((tm, y.shape[1]), lambda i: (i, 0)) for y in ys]
    in_specs += [pl.BlockSpec(w.shape, lambda i: (0, 0)) for w in ws]
    return pl.pallas_call(
        functools.partial(_proj_residual_kernel, n_in=len(ys)),
        grid=(m // tm,), in_specs=in_specs, out_specs=pl.BlockSpec((tm, d), lambda i: (i, 0)),
        out_shape=jax.ShapeDtypeStruct((m, d), F32),
        compiler_params=_cparams("parallel"), name="proj_residual")(x, *ys, *ws)


FFN_CHUNKS = 1


def _ffn_kernel(*refs, final_norm):
    if final_norm:
        x_ref, g_ref, wg_ref, wu_ref, wd_ref, gf_ref, o_ref, y_ref = refs
    else:
        x_ref, g_ref, wg_ref, wu_ref, wd_ref, o_ref = refs
    x = x_ref[...]
    xn = _rmsnorm(x, g_ref[...]).astype(BF16)
    fc = wg_ref.shape[1] // FFN_CHUNKS
    out = x
    for c in range(FFN_CHUNKS):
        cols = slice(c * fc, (c + 1) * fc)
        gate = jnp.dot(xn, wg_ref[:, cols], preferred_element_type=F32)
        up = jnp.dot(xn, wu_ref[:, cols], preferred_element_type=F32)
        h = (gate * _sigmoid(gate) * up).astype(BF16)
        out = out + jnp.dot(h, wd_ref[cols, :], preferred_element_type=F32)
    o_ref[...] = out
    if final_norm:
        y_ref[...] = _rmsnorm(out, gf_ref[...])


def ffn_residual(x, g, wg, wu, wd, final_g=None):
    m, d = x.shape
    tm = _row_tile(m)
    final_norm = final_g is not None
    resident = lambda a: pl.BlockSpec(a.shape, lambda i: (0, 0), pipeline_mode=pl.Buffered(1))
    row = pl.BlockSpec((tm, d), lambda i: (i, 0))
    vec = pl.BlockSpec((1, d), lambda i: (0, 0))
    ins = [x, g.reshape(1, d), wg, wu, wd]
    in_specs = [row, vec, resident(wg), resident(wu), resident(wd)]
    out_shape = [jax.ShapeDtypeStruct((m, d), F32)]
    out_specs = [row]
    if final_norm:
        ins.append(final_g.reshape(1, d))
        in_specs.append(vec)
        out_shape.append(jax.ShapeDtypeStruct((m, d), F32))
        out_specs.append(row)
    res = pl.pallas_call(
        functools.partial(_ffn_kernel, final_norm=final_norm),
        grid=(m // tm,), in_specs=in_specs, out_specs=out_specs, out_shape=out_shape,
        compiler_params=_cparams("parallel"), name="ffn")(*ins)
    return res if final_norm else res[0]


def _head_ones():
    r = _iota2((W_A, W_A), 0) // DH
    c = _iota2((W_A, W_A), 1) // DH
    return (r == c).astype(F32)


def _rwkv_pre_kernel(*refs, seq_blocks, has_prev):
    if has_prev:
        (pa_ref, prev_ref, mu_ref, wlora_ref, gup_ref, w0_ref, a0_ref, kk_ref, ka_ref,
         r_out, lw_out, k_out, v_out, kkn_out, a_out, g_out) = refs
    else:
        (pa_ref, mu_ref, wlora_ref, gup_ref, w0_ref, a0_ref, kk_ref, ka_ref,
         r_out, lw_out, k_out, v_out, kkn_out, a_out, g_out, carry) = refs
    pa = pa_ref[...]
    tm = pa.shape[0]
    if has_prev:
        shifted = prev_ref[...]
    else:
        i = pl.program_id(0)
        first = (i % seq_blocks) == 0
        prev_row = jnp.where(first, 0.0, carry[...])
        rolled = pltpu.roll(pa, 1, 0)
        shifted = jnp.where(_iota2(pa.shape, 0) == 0, prev_row, rolled)
        carry[...] = pa[tm - 1:tm, :]
    z = pa + (shifted - pa) * mu_ref[...]
    r = z[:, 0:W_A]
    k = z[:, W_A:2 * W_A]
    v = z[:, 2 * W_A:3 * W_A]
    lo = z[:, 3 * W_A:3 * W_A + LORA_W + LORA_A]
    gd = z[:, 3 * W_A + LORA_W + LORA_A:]
    lane = _iota2(lo.shape, 1)
    lo = jnp.where(lane < LORA_W, jnp.tanh(lo), lo)
    up = _bdot(lo, wlora_ref[...])
    w = -_softplus(-(w0_ref[...] + up[:, :W_A])) - 0.5
    a = _sigmoid(a0_ref[...] + up[:, W_A:])
    g = _bdot(_sigmoid(gd), gup_ref[...])
    kk = k * kk_ref[...]
    ss = _ones_dot_right(kk * kk, _head_ones())
    kk = kk / jnp.maximum(jnp.sqrt(ss), 1e-12)
    r_out[...] = r
    lw_out[...] = -jnp.exp(w)
    k_out[...] = k * (1.0 + (a - 1.0) * ka_ref[...])
    v_out[...] = v
    kkn_out[...] = kk
    a_out[...] = a
    g_out[...] = g


def rwkv_pre(pa, pa_prev, seq_len, p):
    m = pa.shape[0]
    has_prev = pa_prev is not None
    tm = m if has_prev else min(512, seq_len)
    vec = lambda a: a.reshape(1, -1)
    ins = [pa] + ([pa_prev] if has_prev else []) + [
        vec(p['mu']), p['w_lora'], p['g_up'], vec(p['w0']), vec(p['a0']), vec(p['k_k']), vec(p['k_a'])]
    row = lambda n: pl.BlockSpec((tm, n), lambda i: (i, 0))
    full = lambda a: pl.BlockSpec(a.shape, lambda i: (0, 0))
    in_specs = [row(A_COLS)] + ([row(A_COLS)] if has_prev else []) + [full(a) for a in ins[-7:]]
    scratch = [] if has_prev else [pltpu.VMEM((1, A_COLS), F32)]
    return pl.pallas_call(
        functools.partial(_rwkv_pre_kernel, seq_blocks=max(seq_len // tm, 1), has_prev=has_prev),
        grid=(m // tm,), in_specs=in_specs, out_specs=[row(W_A)] * 7,
        out_shape=[jax.ShapeDtypeStruct((m, W_A), F32)] * 7, scratch_shapes=scratch,
        compiler_params=_cparams("arbitrary"), name="rwkv_pre")(*ins)


def _group_norm_gate(y, r, k, v, g, gng, gnb, rk):
    outs = []
    for h in range(N_HEADS):
        sl = slice(h * DH, (h + 1) * DH)
        yh = y[:, sl]
        mean = jnp.mean(yh, axis=-1, keepdims=True)
        var = jnp.mean(jnp.square(yh - mean), axis=-1, keepdims=True)
        yn = (yh - mean) * lax.rsqrt(var + GN_EPS) * gng[:, sl] + gnb[:, sl]
        bonus = jnp.sum(r[:, sl] * k[:, sl] * rk[:, sl], axis=-1, keepdims=True) * v[:, sl]
        outs.append(yn + bonus)
    return jnp.concatenate(outs, axis=-1) * g


def _rwkv_chunk_kernel(r_ref, lw_ref, k_ref, v_ref, kk_ref, a_ref, g_ref, gng_ref, gnb_ref, rk_ref,
                       y_ref, s_ref, s_scr):
    c = pl.program_id(1)
    C = RWKV_CHUNK

    @pl.when(c == 0)
    def _():
        s_scr[...] = jnp.zeros_like(s_scr)

    tri = (_iota2((C, C), 1) <= _iota2((C, C), 0)).astype(F32)

    n = RWKV_GROUP * C
    gw = RWKV_GROUP * DH
    rb = _iota2((n, gw), 0)
    cb = _iota2((n, gw), 1)
    bdmask = (rb // C) == (cb // DH)
    rb2 = _iota2((n, n), 0) % C
    cb2 = _iota2((n, n), 1) % C
    strict = cb2 < rb2
    lower = cb2 <= rb2

    def bd(x):
        return jnp.where(bdmask, jnp.concatenate([x] * RWKV_GROUP, axis=0), 0.0)

    nb = r_ref.shape[0]
    ng = N_HEADS // RWKV_GROUP
    chains = [(bi, gi) for bi in range(nb) for gi in range(ng)]
    e_last, xar, ybk, vbd, s0 = {}, {}, {}, {}, {}
    for bi in range(nb):
        lw = lw_ref[bi]
        cum = _ones_dot_left(tri, lw)
        e_in = jnp.exp(cum)
        e_neg = jnp.exp(-cum)
        kk = kk_ref[bi]
        a_hat = -kk * jnp.exp(cum - lw)
        b_hat = kk * a_ref[bi] * e_neg
        k_hat = k_ref[bi] * e_neg
        r_hat = r_ref[bi] * e_in
        v = v_ref[bi]
        e_last[bi] = e_in[C - 1:C, :]
        for gi in range(ng):
            sl = slice(gi * gw, (gi + 1) * gw)
            xar[bi, gi] = jnp.concatenate([bd(a_hat[:, sl]), bd(r_hat[:, sl])], axis=0).astype(BF16)
            ybk[bi, gi] = jnp.concatenate([bd(b_hat[:, sl]), bd(k_hat[:, sl])], axis=0).astype(BF16)
            vbd[bi, gi] = bd(v[:, sl]).astype(BF16)
            s0[bi, gi] = s_scr[bi, gi]
    gm = {ch: _bdot_nt(xar[ch], ybk[ch]) for ch in chains}
    xs = {ch: _bdot_nt(xar[ch], s0[ch]) for ch in chains}
    pw = {ch: jnp.where(strict, gm[ch][:n, :n], 0.0) for ch in chains}
    u = {ch: xs[ch][:n] + _bdot(jnp.where(strict, gm[ch][:n, n:], 0.0), vbd[ch]) for ch in chains}
    u = {ch: u[ch] + _bdot(pw[ch], u[ch]) for ch in chains}
    for _ in range(int(math.log2(C)) - 1):
        pw = {ch: _bdot(pw[ch], pw[ch]) for ch in chains}
        u = {ch: u[ch] + _bdot(pw[ch], u[ch]) for ch in chains}
    uv = {ch: jnp.concatenate([u[ch].astype(BF16), vbd[ch]], axis=0) for ch in chains}
    ys = {}
    for ch in chains:
        m_r = jnp.concatenate([jnp.where(lower, gm[ch][n:, :n], 0.0), jnp.where(lower, gm[ch][n:, n:], 0.0)],
                              axis=1)
        y_bd = xs[ch][n:] + _bdot(m_r, uv[ch])
        ys[ch] = sum(y_bd[hh * C:(hh + 1) * C] for hh in range(RWKV_GROUP))
    for bi, gi in chains:
        s_scr[bi, gi] = ((s0[bi, gi] + _bdot_tn(uv[bi, gi], ybk[bi, gi]))
                         * e_last[bi][:, gi * gw:(gi + 1) * gw])
    for bi in range(nb):
        y = jnp.concatenate([ys[bi, gi] for gi in range(ng)], axis=-1)
        y_ref[bi] = _group_norm_gate(y, r_ref[bi], k_ref[bi], v_ref[bi], g_ref[bi], gng_ref[...],
                                     gnb_ref[...], rk_ref[...])

    @pl.when(c == pl.num_programs(1) - 1)
    def _():
        for bi in range(r_ref.shape[0]):
            for h in range(N_HEADS):
                hh = h % RWKV_GROUP
                s_ref[bi, h] = s_scr[bi, h // RWKV_GROUP][hh * DH:(hh + 1) * DH, hh * DH:(hh + 1) * DH]


def rwkv_chunked(pre, batch, seq_len, p):
    C = RWKV_CHUNK
    nc = seq_len // C
    nb = RWKV_SEQS if batch % RWKV_SEQS == 0 else 1
    seq = lambda a: a.reshape(batch, seq_len, W_A)
    blk = pl.BlockSpec((nb, C, W_A), lambda b, c: (b, c, 0))
    vec = lambda a: a.reshape(1, W_A)
    full = pl.BlockSpec((1, W_A), lambda b, c: (0, 0))
    y, s = pl.pallas_call(
        _rwkv_chunk_kernel, grid=(batch // nb, nc),
        in_specs=[blk] * 7 + [full] * 3,
        out_specs=[blk, pl.BlockSpec((nb, N_HEADS, DH, DH), lambda b, c: (b, 0, 0, 0))],
        out_shape=[jax.ShapeDtypeStruct((batch, seq_len, W_A), F32),
                   jax.ShapeDtypeStruct((batch, N_HEADS, DH, DH), F32)],
        scratch_shapes=[pltpu.VMEM((nb, N_HEADS // RWKV_GROUP, RWKV_GROUP * DH, RWKV_GROUP * DH), F32)],
        compiler_params=_cparams("parallel", "arbitrary"), name="rwkv_chunk",
    )(*[seq(a) for a in pre], vec(p['gn_g']), vec(p['gn_b']), vec(p['r_k']))
    return y.reshape(batch * seq_len, W_A), s


def _rwkv_step_kernel(r_ref, lw_ref, k_ref, v_ref, kk_ref, a_ref, g_ref, gng_ref, gnb_ref, rk_ref,
                      s_ref, y_ref, so_ref, vt_scr, yt_scr):
    tr = lambda ref: ref[...].T
    r, k, kk, g = tr(r_ref), tr(k_ref), tr(kk_ref), tr(g_ref)
    v = tr(v_ref)
    vt_scr[...] = v
    w = jnp.exp(tr(lw_ref))
    b = kk * tr(a_ref)
    outs = []
    for hh in range(2):
        sl = slice(hh * DH, (hh + 1) * DH)
        rh, kh, nkk, wh, bh = r[sl], k[sl], -kk[sl], w[sl], b[sl]

        def body(i, carry):
            s = s_ref[hh, i]
            sa = jnp.sum(s * nkk, axis=0, keepdims=True)
            s = s * wh + sa * bh + vt_scr[pl.ds(hh * DH + i, 1), :] * kh
            so_ref[hh, i] = s
            yt_scr[pl.ds(hh * DH + i, 1), :] = jnp.sum(s * rh, axis=0, keepdims=True)
            return carry

        lax.fori_loop(0, DH, body, 0)
        yh = yt_scr[sl, :]
        mean = jnp.mean(yh, axis=0, keepdims=True)
        var = jnp.mean(jnp.square(yh - mean), axis=0, keepdims=True)
        yn = (yh - mean) * lax.rsqrt(var + GN_EPS) * gng_ref[sl, :] + gnb_ref[sl, :]
        bonus = jnp.sum(rh * kh * rk_ref[sl, :], axis=0, keepdims=True) * v[sl]
        outs.append((yn + bonus) * g[sl])
    y_ref[...] = jnp.concatenate(outs, axis=0).T


def rwkv_step(pre, state_t, layer, p):
    batch = state_t.shape[-1]
    pair = 2 * DH
    blk = pl.BlockSpec((batch, pair), lambda h: (0, h))
    col = lambda a: a.reshape(W_A, 1)
    cblk = pl.BlockSpec((pair, 1), lambda h: (h, 0))
    y, s = pl.pallas_call(
        _rwkv_step_kernel, grid=(N_HEADS // 2,),
        in_specs=[blk] * 7 + [cblk] * 3 + [pl.BlockSpec((None, 2, DH, DH, batch), lambda h: (layer, h, 0, 0, 0))],
        out_specs=[blk, pl.BlockSpec((2, DH, DH, batch), lambda h: (h, 0, 0, 0))],
        out_shape=[jax.ShapeDtypeStruct((batch, W_A), F32),
                   jax.ShapeDtypeStruct((N_HEADS, DH, DH, batch), F32)],
        scratch_shapes=[pltpu.VMEM((pair, batch), F32), pltpu.VMEM((pair, batch), F32)],
        compiler_params=_cparams("parallel"), name="rwkv_step",
    )(*pre, col(p['gn_g']), col(p['gn_b']), col(p['r_k']), state_t)
    return y, s


def _fox_gate_kernel(f_ref, bf_ref, lf_ref, c_ref, ct_ref):
    t = f_ref.shape[0]
    lf = _log_sigmoid(f_ref[...] + bf_ref[...])
    lf_ref[...] = lf[:, :N_HEADS]
    blk = LANE
    row = _iota2((blk, blk), 0)
    col = _iota2((blk, blk), 1)
    tri = (col <= row).astype(F32)
    carry = jnp.zeros((1, LANE), F32)
    for i in range(t // blk):
        c_blk = _ones_dot_left(tri, lf[i * blk:(i + 1) * blk]) + carry
        c_ref[i * blk:(i + 1) * blk, :] = c_blk
        ct_ref[i] = c_blk.T[:N_HEADS]
        carry = c_blk[blk - 1:blk, :]


def fox_gates(f_pad, b_f, batch, seq_len):
    bf = jnp.zeros((1, LANE), F32).at[0, :N_HEADS].set(b_f)
    return pl.pallas_call(
        _fox_gate_kernel, grid=(batch,),
        in_specs=[pl.BlockSpec((None, seq_len, LANE), lambda b: (b, 0, 0)),
                  pl.BlockSpec((1, LANE), lambda b: (0, 0))],
        out_specs=[pl.BlockSpec((None, seq_len, N_HEADS), lambda b: (b, 0, 0)),
                   pl.BlockSpec((None, seq_len, LANE), lambda b: (b, 0, 0)),
                   pl.BlockSpec((None, seq_len // LANE, N_HEADS, LANE), lambda b: (b, 0, 0, 0))],
        out_shape=[jax.ShapeDtypeStruct((batch, seq_len, N_HEADS), F32),
                   jax.ShapeDtypeStruct((batch, seq_len, LANE), F32),
                   jax.ShapeDtypeStruct((batch, seq_len // LANE, N_HEADS, LANE), F32)],
        compiler_params=_cparams("parallel"), name="fox_gates",
    )(f_pad.reshape(batch, seq_len, LANE), bf)


def _fox_prompt_kernel(q_ref, k_ref, v_ref, c_ref, ct_ref, o_ref, kb_scr, vlo_scr, vhi_scr, *, pair_axis):
    pr = pl.program_id(pair_axis)
    qi = pl.program_id(2)
    tq = q_ref.shape[0]
    tk = tq
    lane = _iota2((1, LANE), 1)
    lo = lane < DH

    @pl.when(qi == 0)
    def _():
        kb_scr[...] = k_ref[...].astype(BF16)
        v = v_ref[...]
        vlo_scr[...] = jnp.where(lo, v, 0.0).astype(BF16)
        vhi_scr[...] = jnp.where(lo, 0.0, v).astype(BF16)

    q = q_ref[...] * QK_SCALE
    q_lo = jnp.where(lo, q, 0.0).astype(BF16)
    q_hi = jnp.where(lo, 0.0, q).astype(BF16)
    c_all = c_ref[...]
    head_lane = _iota2(c_all.shape, 1)
    cq = [jnp.sum(jnp.where(head_lane == 2 * pr + j, c_all, 0.0), axis=-1, keepdims=True)
          for j in range(2)]
    qs = (q_lo, q_hi)
    vs = (vlo_scr, vhi_scr)

    def step(j, carry, masked):
        m, l, acc = carry
        off = pl.multiple_of(j * tk, tk)
        kb = kb_scr[pl.ds(off, tk), :]
        two = range(2)
        ck = [jnp.concatenate([ct_ref[j * (tk // LANE) + i, pl.ds(2 * pr + hh, 1), :]
                               for i in range(tk // LANE)], axis=-1) for hh in two]
        s = [_bdot_nt(qs[hh], kb) + cq[hh] - ck[hh] for hh in two]
        if masked:
            causal = _iota2(s[0].shape, 1) <= _iota2(s[0].shape, 0)
            s = [jnp.where(causal, s[hh], NEG_INF) for hh in two]
        mn = [jnp.maximum(m[hh], jnp.max(s[hh], axis=-1, keepdims=True)) for hh in two]
        alphas = [jnp.exp(m[hh] - mn[hh]) for hh in two]
        pmat = [jnp.exp(s[hh] - mn[hh]) for hh in two]
        pvs = [jnp.dot(pmat[hh].astype(BF16), vs[hh][pl.ds(off, tk), :], preferred_element_type=F32)
               for hh in two]
        new_l = [l[hh] * alphas[hh] + jnp.sum(pmat[hh], axis=-1, keepdims=True) for hh in two]
        acc = acc * jnp.where(lo, alphas[0], alphas[1]) + pvs[0] + pvs[1]
        return tuple(mn), tuple(new_l), acc

    init = ((jnp.full((tq, 1), NEG_INF, F32),) * 2, (jnp.zeros((tq, 1), F32),) * 2,
            jnp.zeros((tq, LANE), F32))
    carry = lax.fori_loop(0, qi, lambda j, cr: step(j, cr, False), init)
    m, l, acc = step(qi, carry, True)
    o_ref[...] = acc / jnp.where(lo, l[0], l[1])


def fox_prompt(q, k, v, c, ct, batch, seq_len):
    tq = min(ATT_BLOCK, seq_len)
    seq = lambda a: a.reshape(-1, batch, seq_len, W_A)
    last = lambda a: 0 if a.ndim == 2 else a.shape[0] - 1
    kv_spec = lambda a: pl.BlockSpec((None, None, seq_len, LANE),
                                     lambda b, pr, i, ly=last(a): (ly, b, 0, pr))
    o = pl.pallas_call(
        functools.partial(_fox_prompt_kernel, pair_axis=1),
        grid=(batch, N_HEADS // 2, seq_len // tq),
        in_specs=[pl.BlockSpec((None, None, tq, LANE), lambda b, pr, i: (0, b, i, pr)), kv_spec(k), kv_spec(v),
                  pl.BlockSpec((None, tq, LANE), lambda b, pr, i: (b, i, 0)),
                  pl.BlockSpec((None, seq_len // LANE, N_HEADS, LANE), lambda b, pr, i: (b, 0, 0, 0))],
        out_specs=pl.BlockSpec((None, tq, LANE), lambda b, pr, i: (b, i, pr)),
        out_shape=jax.ShapeDtypeStruct((batch, seq_len, W_A), F32),
        scratch_shapes=[pltpu.VMEM((seq_len, LANE), BF16)] * 3,
        compiler_params=_cparams("parallel", "parallel", "arbitrary"), name="fox_prompt",
    )(seq(q), seq(k), seq(v), c, ct)
    return o.reshape(batch * seq_len, W_A)


FOX_PAGES_PER_STEP = 16


def _fox_decode_kernel(pt_ref, q_ref, kn_ref, vn_ref, lfn_ref, *refs, n_pages):
    g = n_pages
    lf_refs = refs[0:g]
    k_refs = refs[g:2 * g]
    v_refs = refs[2 * g:3 * g]
    o_ref = refs[3 * g]
    qb_scr, m_scr, l_scr, acc_scr, tail_scr = refs[3 * g + 1:]
    step = pl.program_id(1)
    page = k_refs[0].shape[-1]
    eye = _iota2((DH, DH), 0) == _iota2((DH, DH), 1)

    @pl.when(step == 0)
    def _():
        m_scr[...] = jnp.full_like(m_scr, NEG_INF)
        l_scr[...] = jnp.zeros_like(l_scr)
        acc_scr[...] = jnp.zeros_like(acc_scr)
        tail_scr[...] = jnp.zeros_like(tail_scr)
        q = q_ref[...] * QK_SCALE
        for h in range(N_HEADS):
            q_col = jnp.sum(jnp.where(eye, q[:, h * DH:(h + 1) * DH], 0.0), axis=-1, keepdims=True)
            qb_scr[h] = jnp.broadcast_to(q_col, (DH, page))

    lfn = lfn_ref[...]
    after = (_iota2((page, page), 0) > _iota2((page, page), 1)).astype(F32)
    tail = tail_scr[...]
    decs = []
    for i in range(g):
        lf = lf_refs[i][...]
        decs.append(_ones_dot_right(lf, after) + tail + lfn)
        tail = tail + jnp.sum(lf, axis=-1, keepdims=True)
    tail_scr[...] = tail
    dec = jnp.concatenate(decs, axis=-1)
    for h in range(N_HEADS):
        qb = qb_scr[h]
        s = jnp.concatenate([jnp.sum(qb * k_refs[i][h], axis=0, keepdims=True) for i in range(g)],
                            axis=-1) + dec[h:h + 1, :]
        m_old = m_scr[h][:, :1]
        mn = jnp.maximum(m_old, jnp.max(s, axis=-1, keepdims=True))
        alpha = jnp.exp(m_old - mn)
        pmat = jnp.exp(s - mn)
        l_new = l_scr[h][:, :1] * alpha + jnp.sum(pmat, axis=-1, keepdims=True)
        m_scr[h] = jnp.broadcast_to(mn, (1, LANE))
        l_scr[h] = jnp.broadcast_to(l_new, (1, LANE))
        acc = acc_scr[h] * alpha
        for i in range(g):
            acc = acc + pmat[:, i * page:(i + 1) * page] * v_refs[i][h]
        acc_scr[h] = acc

    @pl.when(step == pl.num_programs(1) - 1)
    def _():
        q = q_ref[...] * QK_SCALE
        kn = kn_ref[...]
        vn = vn_ref[...]
        outs = []
        for h in range(N_HEADS):
            sl = slice(h * DH, (h + 1) * DH)
            s_new = jnp.sum(q[:, sl] * kn[:, sl], axis=-1, keepdims=True)
            m_old = m_scr[h][:, :1]
            mn = jnp.maximum(m_old, s_new)
            alpha = jnp.exp(m_old - mn)
            p_new = jnp.exp(s_new - mn)
            lt = l_scr[h][:, :1] * alpha + p_new
            o_col = jnp.sum(acc_scr[h], axis=-1, keepdims=True) * alpha
            o_row = jnp.sum(jnp.where(eye, o_col, 0.0), axis=0, keepdims=True)
            outs.append((o_row + p_new * vn[:, sl]) / lt)
        o_ref[...] = jnp.concatenate(outs, axis=-1)


def fox_decode(q, k_new, v_new, logf_new, cache_kt, cache_vt, cache_lf_t, layer, page_table):
    batch, n_pages = page_table.shape
    page = cache_kt.shape[-1]
    g = min(FOX_PAGES_PER_STEP, n_pages)
    steps = n_pages // g
    tok = lambda a: a.reshape(batch, 1, W_A)
    tok_spec = pl.BlockSpec((None, 1, W_A), lambda b, s, pt: (b, 0, 0))

    def page_idx(b, s, pt, i):
        return pt[b * n_pages + (n_pages - 1 - (s * g + i))]

    lf_specs = [pl.BlockSpec((None, None, N_HEADS, page),
                             lambda b, s, pt, i=i: (layer, page_idx(b, s, pt, i), 0, 0)) for i in range(g)]
    kv_specs = [pl.BlockSpec((None, None, N_HEADS, DH, page),
                             lambda b, s, pt, i=i: (layer, page_idx(b, s, pt, i), 0, 0, 0)) for i in range(g)]
    grid_spec = pltpu.PrefetchScalarGridSpec(
        num_scalar_prefetch=1, grid=(batch, steps),
        in_specs=[tok_spec, tok_spec, tok_spec,
                  pl.BlockSpec((None, N_HEADS, 1), lambda b, s, pt: (b, 0, 0))] + lf_specs + kv_specs + kv_specs,
        out_specs=tok_spec,
        scratch_shapes=[pltpu.VMEM((N_HEADS, DH, page), F32), pltpu.VMEM((N_HEADS, 1, LANE), F32),
                        pltpu.VMEM((N_HEADS, 1, LANE), F32), pltpu.VMEM((N_HEADS, DH, page), F32),
                        pltpu.VMEM((N_HEADS, 1), F32)])
    o = pl.pallas_call(
        functools.partial(_fox_decode_kernel, n_pages=g), grid_spec=grid_spec,
        out_shape=jax.ShapeDtypeStruct((batch, 1, W_A), F32),
        compiler_params=_cparams("parallel", "arbitrary"), name="fox_decode",
    )(page_table.reshape(-1), tok(q), tok(k_new), tok(v_new), logf_new.reshape(batch, N_HEADS, 1),
      *([cache_lf_t] * g), *([cache_kt] * g), *([cache_vt] * g))
    return o.reshape(batch, W_A)


def _t5_buckets(dist):
    n = np.maximum(dist, 0)
    max_exact = N_BUCKETS // 2
    nf = np.maximum(n, 1).astype(np.float32)
    large = max_exact + (np.log(nf / max_exact) / math.log(MAX_DISTANCE / max_exact)
                         * (N_BUCKETS - max_exact)).astype(np.int32)
    large = np.minimum(large, N_BUCKETS - 1)
    return np.where(n < max_exact, n, large).astype(np.int32)


def _bias_gather_kernel(idx_ref, table_ref, o_ref):
    col = pl.program_id(0)
    idx = idx_ref[...]
    out = jnp.zeros(idx.shape, F32)
    for bkt in range(N_BUCKETS):
        out = jnp.where(idx == bkt, table_ref[bkt, col], out)
    o_ref[...] = out


def bias_gather(idx, table):
    r, c = idx.shape
    ncol = table.shape[1]
    return pl.pallas_call(
        _bias_gather_kernel, grid=(ncol,),
        in_specs=[pl.BlockSpec((r, c), lambda j: (0, 0)),
                  pl.BlockSpec(memory_space=pltpu.SMEM)],
        out_specs=pl.BlockSpec((None, r, c), lambda j: (j, 0, 0)),
        out_shape=jax.ShapeDtypeStruct((ncol, r, c), F32),
        compiler_params=_cparams("parallel"), name="bias_gather")(idx, table)


def _diff_lambda(lq1, lk1, lq2, lk2, lambda_init):
    return (jnp.exp(jnp.sum(lq1 * lk1, axis=-1, keepdims=True))
            - jnp.exp(jnp.sum(lq2 * lk2, axis=-1, keepdims=True)) + lambda_init)


def _diff_prompt_kernel(q_ref, k_ref, v_ref, bias_ref, far_ref, lam_ref, sub_ref, o_ref, kb_scr, vb_scr,
                        *, lambda_init):
    qi = pl.program_id(2)
    tq = q_ref.shape[0]
    tk = tq
    lane = _iota2((1, LANE), 1)
    lo = lane < DH

    @pl.when(qi == 0)
    def _():
        kb_scr[...] = k_ref[...].astype(BF16)
        vb_scr[...] = v_ref[...].astype(BF16)

    q = q_ref[...] * QK_SCALE
    qs = (jnp.where(lo, q, 0.0).astype(BF16), jnp.where(lo, 0.0, q).astype(BF16))
    head = pl.program_id(1)
    fars = (far_ref[head, 0], far_ref[head, 1])

    def step(j, carry, kind):
        m, l, acc = carry
        off = pl.multiple_of(j * tk, tk)
        kb = kb_scr[pl.ds(off, tk), :]
        vb = vb_scr[pl.ds(off, tk), :]
        two = range(2)
        s = [_bdot_nt(qs[mp], kb) for mp in two]
        if kind == 'far':
            s = [s[mp] + fars[mp] for mp in two]
        elif kind == 'near':
            s = [s[mp] + bias_ref[mp, 1] for mp in two]
        else:
            causal = _iota2(s[0].shape, 1) <= _iota2(s[0].shape, 0)
            s = [jnp.where(causal, s[mp] + bias_ref[mp, 0], NEG_INF) for mp in two]
        mn = [jnp.maximum(m[mp], jnp.max(s[mp], axis=-1, keepdims=True)) for mp in two]
        alpha = [jnp.exp(m[mp] - mn[mp]) for mp in two]
        pmat = [jnp.exp(s[mp] - mn[mp]) for mp in two]
        pv = [jnp.dot(pmat[mp].astype(BF16), vb, preferred_element_type=F32) for mp in two]
        new_l = [l[mp] * alpha[mp] + jnp.sum(pmat[mp], axis=-1, keepdims=True) for mp in two]
        new_acc = [acc[mp] * alpha[mp] + pv[mp] for mp in two]
        return tuple(mn), tuple(new_l), tuple(new_acc)

    init = ((jnp.full((tq, 1), NEG_INF, F32),) * 2, (jnp.zeros((tq, 1), F32),) * 2,
            (jnp.zeros((tq, LANE), F32),) * 2)
    carry = lax.fori_loop(0, jnp.maximum(qi - 1, 0), lambda j, cr: step(j, cr, 'far'), init)
    carry = lax.cond(qi >= 1, lambda cr: step(qi - 1, cr, 'near'), lambda cr: cr, carry)
    m, l, acc = step(qi, carry, 'diag')
    lam = _diff_lambda(lam_ref[0:1, :], lam_ref[1:2, :], lam_ref[2:3, :], lam_ref[3:4, :], lambda_init)
    o = acc[0] / l[0] - lam * (acc[1] / l[1])
    o_ref[...] = _rmsnorm(o, sub_ref[...]) * (1.0 - lambda_init)


def diff_prompt(q, k, v, bias_tiles, far_bias, lam_vecs, subln_g, lambda_init, batch, seq_len):
    tq = min(ATT_BLOCK, seq_len)
    seq = lambda a: a.reshape(-1, batch, seq_len, W_C)
    last = lambda a: 0 if a.ndim == 2 else a.shape[0] - 1
    kv_spec = lambda a: pl.BlockSpec((None, None, seq_len, LANE),
                                     lambda b, h, i, ly=last(a): (ly, b, 0, h))
    blk = pl.BlockSpec((None, tq, LANE), lambda b, h, i: (b, i, h))
    o = pl.pallas_call(
        functools.partial(_diff_prompt_kernel, lambda_init=lambda_init),
        grid=(batch, N_HEADS, seq_len // tq),
        in_specs=[pl.BlockSpec((None, None, tq, LANE), lambda b, h, i: (0, b, i, h)), kv_spec(k), kv_spec(v),
                  pl.BlockSpec((None, 2, 2, tq, tq), lambda b, h, i: (h, 0, 0, 0, 0)),
                  pl.BlockSpec(memory_space=pltpu.SMEM),
                  pl.BlockSpec((4, DH), lambda b, h, i: (0, 0)),
                  pl.BlockSpec((1, LANE), lambda b, h, i: (0, 0))],
        out_specs=blk,
        out_shape=jax.ShapeDtypeStruct((batch, seq_len, W_C), F32),
        scratch_shapes=[pltpu.VMEM((seq_len, LANE), BF16)] * 2,
        compiler_params=_cparams("parallel", "parallel", "arbitrary"), name="diff_prompt",
    )(seq(q), seq(k), seq(v), bias_tiles, far_bias, lam_vecs, subln_g.reshape(1, LANE))
    return o.reshape(batch * seq_len, W_C)


DIFF_PAGES_PER_STEP = 8


def _diff_decode_kernel(pt_ref, q_ref, kn_ref, vn_ref, bias_ref, rel_ref, lam_ref, sub_ref, *refs,
                        n_pages, lambda_init):
    g = n_pages
    k_refs = refs[0:g]
    v_refs = refs[g:2 * g]
    o_ref = refs[2 * g]
    m_scr, l_scr, acc_scr = refs[2 * g + 1:]
    step = pl.program_id(1)

    @pl.when(step == 0)
    def _():
        m_scr[...] = jnp.full_like(m_scr, NEG_INF)
        l_scr[...] = jnp.zeros_like(l_scr)
        acc_scr[...] = jnp.zeros_like(acc_scr)

    row = _iota2((N_HEADS, LANE), 0)
    lane = _iota2((N_HEADS, LANE), 1)
    qsel = ((row == 0) & (lane < DH)) | ((row == 1) & (lane >= DH))

    def q_tile(h):
        return jnp.where(qsel, q_ref[h:h + 1, :] * QK_SCALE, 0.0)

    page = k_refs[0].shape[0] // N_HEADS
    heads = range(N_HEADS)

    def head_rows(refs_, h):
        return jnp.concatenate([r_[pl.ds(h, page, stride=N_HEADS), :] for r_ in refs_], axis=0)

    s = [lax.dot_general(q_tile(h), head_rows(k_refs, h), (((1,), (1,)), ((), ())),
                         preferred_element_type=F32) + bias_ref[h] for h in heads]
    m_old = [m_scr[h][:, :1] for h in heads]
    mn = [jnp.maximum(m_old[h], jnp.max(s[h], axis=-1, keepdims=True)) for h in heads]
    alpha = [jnp.exp(m_old[h] - mn[h]) for h in heads]
    pmat = [jnp.exp(s[h] - mn[h]) for h in heads]
    pv = [jnp.dot(pmat[h], head_rows(v_refs, h), preferred_element_type=F32) for h in heads]
    for h in heads:
        l_new = l_scr[h][:, :1] * alpha[h] + jnp.sum(pmat[h], axis=-1, keepdims=True)
        m_scr[h] = jnp.broadcast_to(mn[h], (N_HEADS, LANE))
        l_scr[h] = jnp.broadcast_to(l_new, (N_HEADS, LANE))
        acc_scr[h] = acc_scr[h] * alpha[h] + pv[h]

    @pl.when(step == pl.num_programs(1) - 1)
    def _():
        lam = _diff_lambda(lam_ref[0:1, :], lam_ref[1:2, :], lam_ref[2:3, :], lam_ref[3:4, :], lambda_init)
        for h in range(N_HEADS):
            b_new = jnp.where(row[:, :1] == 0, rel_ref[0, 2 * h], rel_ref[0, 2 * h + 1])
            s_new = jnp.sum(q_tile(h) * kn_ref[h:h + 1, :], axis=-1, keepdims=True) + b_new
            m_old = m_scr[h][:, :1]
            mn = jnp.maximum(m_old, s_new)
            alpha = jnp.exp(m_old - mn)
            p_new = jnp.exp(s_new - mn)
            lt = l_scr[h][:, :1] * alpha + p_new
            o = (acc_scr[h] * alpha + p_new * vn_ref[h:h + 1, :]) / lt
            od = o[0:1] - lam * o[1:2]
            od = od * lax.rsqrt(jnp.mean(od * od, axis=-1, keepdims=True) + RMS_EPS)
            o_ref[:, h * LANE:(h + 1) * LANE] = od * sub_ref[...] * (1.0 - lambda_init)


def diff_decode(q, k_new, v_new, cache_k, cache_v, layer, page_table, page_bias, rel_bias, lam_vecs,
                subln_g, lambda_init):
    batch, n_pages = page_table.shape
    page = cache_k.shape[2]
    g = min(DIFF_PAGES_PER_STEP, n_pages)
    steps = n_pages // g
    heads = lambda a: a.reshape(batch, N_HEADS, LANE)
    head_spec = pl.BlockSpec((None, N_HEADS, LANE), lambda b, s, pt: (b, 0, 0))
    pool = cache_k.shape[1]
    cache_k = cache_k.reshape(-1, pool, page * N_HEADS, LANE)
    cache_v = cache_v.reshape(-1, pool, page * N_HEADS, LANE)
    kv_specs = [pl.BlockSpec((None, None, page * N_HEADS, LANE),
                             lambda b, s, pt, i=i: (layer, pt[b * n_pages + s * g + i], 0, 0))
                for i in range(g)]
    grid_spec = pltpu.PrefetchScalarGridSpec(
        num_scalar_prefetch=1, grid=(batch, steps),
        in_specs=[head_spec, head_spec, head_spec,
                  pl.BlockSpec((None, N_HEADS, N_HEADS, g * page), lambda b, s, pt: (s, 0, 0, 0)),
                  pl.BlockSpec(memory_space=pltpu.SMEM),
                  pl.BlockSpec((4, DH), lambda b, s, pt: (0, 0)),
                  pl.BlockSpec((1, LANE), lambda b, s, pt: (0, 0))] + kv_specs + kv_specs,
        out_specs=pl.BlockSpec((None, 1, W_C), lambda b, s, pt: (b, 0, 0)),
        scratch_shapes=[pltpu.VMEM((N_HEADS, N_HEADS, LANE), F32)] * 3)
    o = pl.pallas_call(
        functools.partial(_diff_decode_kernel, n_pages=g, lambda_init=lambda_init), grid_spec=grid_spec,
        out_shape=jax.ShapeDtypeStruct((batch, 1, W_C), F32),
        compiler_params=_cparams("parallel", "arbitrary"), name="diff_decode",
    )(page_table.reshape(-1), heads(q), heads(k_new), heads(v_new), page_bias, rel_bias, lam_vecs,
      subln_g.reshape(1, LANE), *([cache_k] * g), *([cache_v] * g))
    return o.reshape(batch, W_C)


def _prep_params(norm_mix, norm_ffn, norm_final, w_in_ab, w_out_ab, rwkv_mu, rwkv_w0, rwkv_w_up, rwkv_a0,
                 rwkv_a_up, rwkv_g_up, rwkv_k_k, rwkv_k_a, rwkv_r_k, rwkv_gn_g, rwkv_gn_b, fox_b_f, w_in_c,
                 w_out_c, diff_lq1, diff_lk1, diff_lq2, diff_lk2, diff_subln, rel_bias, ffn_gate, ffn_up,
                 ffn_down):
    depth = norm_mix.shape[0]
    o_r, o_wd, o_k, o_v, o_ad, o_gd = np.cumsum([0, W_A, LORA_W, W_A, W_A, LORA_A]).tolist()
    perm = np.concatenate([np.arange(o_r, o_r + W_A), np.arange(o_k, o_k + W_A), np.arange(o_v, o_v + W_A),
                           np.arange(o_wd, o_wd + LORA_W), np.arange(o_ad, o_ad + LORA_A),
                           np.arange(o_gd, o_gd + LORA_G)])
    layers = []
    for li in range(depth):
        j = li // 2
        lp = {'norm_mix': norm_mix[li], 'norm_ffn': norm_ffn[li],
              'ffn_gate': ffn_gate[li].astype(BF16), 'ffn_up': ffn_up[li].astype(BF16),
              'ffn_down': ffn_down[li].astype(BF16)}
        if li % 2 == 0:
            w = w_in_ab[j]
            wa = w[:, :A_COLS][:, perm]
            wb = w[:, A_COLS:]
            wf = jnp.pad(wb[:, 3 * W_A:], ((0, 0), (0, LANE - N_HEADS)))
            lp['w_in'] = [wa.astype(BF16), wb[:, :W_A].astype(BF16), wb[:, W_A:2 * W_A].astype(BF16),
                          wb[:, 2 * W_A:3 * W_A].astype(BF16), wf.astype(BF16)]
            lp['w_out'] = [w_out_ab[j][:W_A].astype(BF16), w_out_ab[j][W_A:].astype(BF16)]
            zeros = jnp.zeros((LORA_W, W_A), F32)
            w_lora = jnp.concatenate([jnp.concatenate([rwkv_w_up[j], zeros], axis=1),
                                      jnp.concatenate([zeros, rwkv_a_up[j]], axis=1)], axis=0)
            lp['rwkv'] = {'mu': rwkv_mu[j][perm], 'w_lora': w_lora.astype(BF16),
                          'g_up': rwkv_g_up[j].astype(BF16), 'w0': rwkv_w0[j], 'a0': rwkv_a0[j],
                          'k_k': rwkv_k_k[j], 'k_a': rwkv_k_a[j], 'r_k': rwkv_r_k[j].reshape(-1),
                          'gn_g': rwkv_gn_g[j], 'gn_b': rwkv_gn_b[j]}
            lp['fox_b_f'] = fox_b_f[j]
        else:
            w = w_in_c[j]
            lp['w_in'] = [w[:, :W_C].astype(BF16), w[:, W_C:2 * W_C].astype(BF16), w[:, 2 * W_C:].astype(BF16)]
            lp['w_out'] = [w_out_c[j].astype(BF16)]
            lp['lam_vecs'] = jnp.stack([diff_lq1[j], diff_lk1[j], diff_lq2[j], diff_lk2[j]])
            lp['subln'] = diff_subln[j]
            lp['lambda_init'] = 0.8 - 0.6 * math.exp(-0.3 * li)
        layers.append(lp)
    return layers, norm_final, rel_bias


def _prompt_bias(rel_bias, seq_len):
    tq = min(ATT_BLOCK, seq_len)
    d = np.arange(tq)[:, None] - np.arange(tq)[None, :]
    idx = np.concatenate([_t5_buckets(d), _t5_buckets(d + tq)], axis=0)
    tiles = bias_gather(jnp.asarray(idx), rel_bias)
    tiles = tiles.reshape(N_HEADS, 2, 2, tq, tq)
    far_idx = np.full((8, LANE), int(_t5_buckets(np.array([2 * tq]))[0]), np.int32)
    far = bias_gather(jnp.asarray(far_idx), rel_bias)[:, 0, 0].reshape(N_HEADS, 2)
    return tiles, far


def _decode_bias(rel_bias, n_pages, page, g):
    t0 = n_pages * page
    steps = n_pages // g
    rows = -(-steps // 8) * 8
    idx = np.zeros((rows, g * page), np.int32)
    idx[:steps] = _t5_buckets(t0 - np.arange(t0).reshape(steps, g * page))
    bias = bias_gather(jnp.asarray(idx), rel_bias)[:, :steps]
    bias = jnp.transpose(bias.reshape(N_HEADS, 2, steps, g * page), (2, 0, 1, 3))
    return jnp.pad(bias, ((0, 0), (0, 0), (0, N_HEADS - 2), (0, 0)))


def _decode_logf_kernel(f_ref, bf_ref, o_ref):
    o_ref[...] = _log_sigmoid(f_ref[...] + bf_ref[...])[:, :N_HEADS]


def _decode_logf(f_pad, b_f):
    bf = jnp.zeros((1, LANE), F32).at[0, :N_HEADS].set(b_f)
    return pl.pallas_call(
        _decode_logf_kernel, out_shape=jax.ShapeDtypeStruct((f_pad.shape[0], N_HEADS), F32),
        name="decode_logf")(f_pad, bf)


def _trunk(x, layers, norm_final, batch, seq_len, past, prompt_bias=None):
    fk, fv, fl, dk, dv, rs, sh = [], [], [], [], [], [], []
    y = None
    stack_kv = past is None and len(layers) == 4
    for li, lp in enumerate(layers):
        j = li // 2
        if li % 2 == 0:
            stack = {2: fk[0], 3: fv[0]} if (stack_kv and j == 1) else None
            pa, q, k, v, f_pad, xn = norm_proj(x, lp['norm_mix'], lp['w_in'], emit_xn=True, stack_prev=stack)
            if past is None:
                pre = rwkv_pre(pa, None, seq_len, lp['rwkv'])
                ya, s_fin = rwkv_chunked(pre, batch, seq_len, lp['rwkv'])
                logf, c, ct = fox_gates(f_pad, lp['fox_b_f'], batch, seq_len)
                yb = fox_prompt(q, k, v, c, ct, batch, seq_len)
            else:
                (pa_prev,) = norm_proj(past['state_shift'][j], None, lp['w_in'][:1])
                pre = rwkv_pre(pa, pa_prev, seq_len, lp['rwkv'])
                ya, s_fin = rwkv_step(pre, past['state_rwkv_t'], j, lp['rwkv'])
                logf = _decode_logf(f_pad, lp['fox_b_f'])
                yb = fox_decode(q, k, v, logf, past['cache_fox_kt'], past['cache_fox_vt'],
                                past['cache_fox_logf_t'], j, past['page_table'])
                logf = logf.reshape(batch, 1, N_HEADS)
            x = proj_residual(x, [ya, yb], lp['w_out'])
            fk.append(k)
            fv.append(v)
            fl.append(logf)
            rs.append(s_fin)
            sh.append(xn.reshape(batch, seq_len, D_MODEL)[:, -1])
        else:
            stack = {1: dk[0], 2: dv[0]} if (stack_kv and j == 1) else None
            q, k, v = norm_proj(x, lp['norm_mix'], lp['w_in'], stack_prev=stack)
            if past is None:
                yc = diff_prompt(q, k, v, prompt_bias[0], prompt_bias[1], lp['lam_vecs'],
                                 lp['subln'], lp['lambda_init'], batch, seq_len)
            else:
                yc = diff_decode(q, k, v, past['cache_diff_k'], past['cache_diff_v'], j, past['page_table'],
                                 past['page_bias'], past['rel_bias'], lp['lam_vecs'], lp['subln'],
                                 lp['lambda_init'])
            x = proj_residual(x, [yc], lp['w_out'])
            dk.append(k)
            dv.append(v)
        if li == len(layers) - 1:
            x, y = ffn_residual(x, lp['norm_ffn'], lp['ffn_gate'], lp['ffn_up'], lp['ffn_down'], norm_final)
        else:
            x = ffn_residual(x, lp['norm_ffn'], lp['ffn_gate'], lp['ffn_up'], lp['ffn_down'])
    rs = jnp.stack(rs)
    if past is not None:
        rs = jnp.transpose(rs, (0, 4, 1, 2, 3))
    stacked = (lambda lst: lst[-1]) if stack_kv else jnp.stack
    heads = lambda a, w: a.reshape(a.shape[0], batch, seq_len, N_HEADS, w)
    return (y.reshape(batch, seq_len, D_MODEL), heads(stacked(fk), DH), heads(stacked(fv), DH), jnp.stack(fl),
            heads(stacked(dk), 2 * DH), heads(stacked(dv), 2 * DH), rs, jnp.stack(sh))


def kernel(x_prompt, x_sample, cache_fox_k, cache_fox_v, cache_fox_logf, cache_diff_k, cache_diff_v, state_rwkv, state_shift, page_table, norm_mix, norm_ffn, norm_final, w_in_ab, w_out_ab, rwkv_mu, rwkv_w0, rwkv_w_up, rwkv_a0, rwkv_a_up, rwkv_g_up, rwkv_k_k, rwkv_k_a, rwkv_r_k, rwkv_gn_g, rwkv_gn_b, fox_b_f, w_in_c, w_out_c, diff_lq1, diff_lk1, diff_lq2, diff_lk2, diff_subln, rel_bias, ffn_gate, ffn_up, ffn_down):
    layers, norm_final, rel_bias = _prep_params(
        norm_mix, norm_ffn, norm_final, w_in_ab, w_out_ab, rwkv_mu, rwkv_w0, rwkv_w_up, rwkv_a0, rwkv_a_up,
        rwkv_g_up, rwkv_k_k, rwkv_k_a, rwkv_r_k, rwkv_gn_g, rwkv_gn_b, fox_b_f, w_in_c, w_out_c, diff_lq1,
        diff_lk1, diff_lq2, diff_lk2, diff_subln, rel_bias, ffn_gate, ffn_up, ffn_down)
    bp, tp, d = x_prompt.shape
    bs, ts, _ = x_sample.shape
    n_pages = page_table.shape[1]
    page = cache_fox_k.shape[2]
    past = {
        'state_shift': state_shift, 'page_table': page_table,
        'state_rwkv_t': jnp.transpose(state_rwkv, (0, 2, 3, 4, 1)),
        'cache_fox_kt': jnp.transpose(cache_fox_k, (0, 1, 3, 4, 2)),
        'cache_fox_vt': jnp.transpose(cache_fox_v, (0, 1, 3, 4, 2)),
        'cache_fox_logf_t': jnp.swapaxes(cache_fox_logf, 2, 3),
        'cache_diff_k': cache_diff_k, 'cache_diff_v': cache_diff_v, 'rel_bias': rel_bias,
        'page_bias': _decode_bias(rel_bias, n_pages, page, min(DIFF_PAGES_PER_STEP, n_pages)),
    }
    out_p = _trunk(x_prompt.reshape(bp * tp, d), layers, norm_final, bp, tp, None,
                   _prompt_bias(rel_bias, tp))
    out_s = _trunk(x_sample.reshape(bs * ts, d), layers, norm_final, bs, ts, past)
    return (out_p[0], out_s[0]) + out_p[1:] + out_s[1:]
```

```python
import functools
import math

import jax
import jax.numpy as jnp
import numpy as np
from jax import lax
from jax.experimental import pallas as pl
from jax.experimental.pallas import tpu as pltpu

F32 = jnp.float32
BF16 = jnp.bfloat16

D_MODEL = 1024
N_HEADS = 8
DH = 64
W_A = N_HEADS * DH
LORA_W = 64
LORA_A = 64
LORA_G = 128
A_COLS = 3 * W_A + LORA_W + LORA_A + LORA_G
W_C = N_HEADS * 2 * DH
N_BUCKETS = 32
MAX_DISTANCE = 128
RMS_EPS = 1e-6
GN_EPS = 64e-5
NEG_INF = -1e30
QK_SCALE = DH ** -0.5

LANE = 128
RWKV_CHUNK = 64
RWKV_GROUP = 4
RWKV_SEQS = 4
ATT_BLOCK = 512
VMEM_LIMIT = 56 * 1024 * 1024


def _cparams(*sem):
    return pltpu.CompilerParams(dimension_semantics=sem, vmem_limit_bytes=VMEM_LIMIT)


def _bdot(a, b):
    return jnp.dot(a.astype(BF16), b.astype(BF16), preferred_element_type=F32)


def _bdot_nt(a, b):
    return lax.dot_general(a.astype(BF16), b.astype(BF16), (((1,), (1,)), ((), ())),
                           preferred_element_type=F32)


def _bdot_tn(a, b):
    return lax.dot_general(a.astype(BF16), b.astype(BF16), (((0,), (0,)), ((), ())),
                           preferred_element_type=F32)


def _split3(x):
    hi = x.astype(BF16).astype(F32)
    r1 = x - hi
    mid = r1.astype(BF16).astype(F32)
    lo = (r1 - mid).astype(BF16).astype(F32)
    return hi, mid, lo


def _ones_dot_left(ones_mat, x):
    n = x.shape[1]
    stack = jnp.concatenate(_split3(x), axis=1).astype(BF16)
    out = jnp.dot(ones_mat.astype(BF16), stack, preferred_element_type=F32)
    return out[:, :n] + out[:, n:2 * n] + out[:, 2 * n:]


def _ones_dot_right(x, ones_mat):
    m = x.shape[0]
    stack = jnp.concatenate(_split3(x), axis=0).astype(BF16)
    out = jnp.dot(stack, ones_mat.astype(BF16), preferred_element_type=F32)
    return out[:m] + out[m:2 * m] + out[2 * m:]


def _sigmoid(x):
    return 1.0 / (1.0 + jnp.exp(-x))


def _log_sigmoid(x):
    return jnp.minimum(x, 0.0) - jnp.log(1.0 + jnp.exp(-jnp.abs(x)))


def _softplus(x):
    return jnp.maximum(x, 0.0) + jnp.log(1.0 + jnp.exp(-jnp.abs(x)))


def _rmsnorm(x, g):
    return x * lax.rsqrt(jnp.mean(x * x, axis=-1, keepdims=True) + RMS_EPS) * g


def _iota2(shape, dim):
    return lax.broadcasted_iota(jnp.int32, shape, dim)


def _row_tile(m):
    for t in (512, 256, 128):
        if m % t == 0:
            return t
    return m


def _norm_proj_kernel(*refs, n_out, with_norm, emit_xn, stacked):
    x_ref = refs[0]
    pos = 1
    if with_norm:
        g_ref = refs[pos]
        pos += 1
    w_refs = refs[pos:pos + n_out]
    pos += n_out
    prev_refs = dict(zip(stacked, refs[pos:pos + len(stacked)]))
    pos += len(stacked)
    out_refs = refs[pos:pos + n_out]
    pos += n_out
    x = x_ref[...]
    if with_norm:
        x = _rmsnorm(x, g_ref[...])
        if emit_xn:
            refs[pos][...] = x
    xb = x.astype(BF16)
    for idx, (w_ref, o_ref) in enumerate(zip(w_refs, out_refs)):
        res = jnp.dot(xb, w_ref[...], preferred_element_type=F32)
        if idx in prev_refs:
            o_ref[0] = prev_refs[idx][...]
            o_ref[1] = res
        else:
            o_ref[...] = res


def norm_proj(x, g, ws, emit_xn=False, stack_prev=None):
    m, d = x.shape
    tm = _row_tile(m)
    with_norm = g is not None
    stack_prev = stack_prev or {}
    stacked = tuple(sorted(stack_prev))
    ins = [x]
    in_specs = [pl.BlockSpec((tm, d), lambda i: (i, 0))]
    if with_norm:
        ins.append(g.reshape(1, d))
        in_specs.append(pl.BlockSpec((1, d), lambda i: (0, 0)))
    for w in ws:
        ins.append(w)
        in_specs.append(pl.BlockSpec(w.shape, lambda i: (0, 0), pipeline_mode=pl.Buffered(1)))
    for idx in stacked:
        ins.append(stack_prev[idx])
        in_specs.append(pl.BlockSpec((tm, ws[idx].shape[1]), lambda i: (i, 0)))
    out_shape, out_specs = [], []
    for idx, w in enumerate(ws):
        n = w.shape[1]
        if idx in stack_prev:
            out_shape.append(jax.ShapeDtypeStruct((2, m, n), F32))
            out_specs.append(pl.BlockSpec((2, tm, n), lambda i: (0, i, 0)))
        else:
            out_shape.append(jax.ShapeDtypeStruct((m, n), F32))
            out_specs.append(pl.BlockSpec((tm, n), lambda i: (i, 0)))
    if emit_xn:
        out_shape.append(jax.ShapeDtypeStruct((m, d), F32))
        out_specs.append(pl.BlockSpec((tm, d), lambda i: (i, 0)))
    return pl.pallas_call(
        functools.partial(_norm_proj_kernel, n_out=len(ws), with_norm=with_norm, emit_xn=emit_xn,
                          stacked=stacked),
        grid=(m // tm,), in_specs=in_specs, out_specs=out_specs, out_shape=out_shape,
        compiler_params=_cparams("parallel"), name="norm_proj")(*ins)


def _proj_residual_kernel(*refs, n_in):
    x_ref = refs[0]
    y_refs = refs[1:1 + n_in]
    w_refs = refs[1 + n_in:1 + 2 * n_in]
    o_ref = refs[1 + 2 * n_in]
    acc = x_ref[...]
    for y_ref, w_ref in zip(y_refs, w_refs):
        acc = acc + jnp.dot(y_ref[...].astype(BF16), w_ref[...], preferred_element_type=F32)
    o_ref[...] = acc


def proj_residual(x, ys, ws):
    m, d = x.shape
    tm = _row_tile(m)
    in_specs = [pl.BlockSpec((tm, d), lambda i: (i, 0))]
    in_specs += [pl.BlockSpec((tm, y.shape[1]), lambda i: (i, 0)) for y in ys]
    in_specs += [pl.BlockSpec(w.shape, lambda i: (0, 0)) for w in ws]
    return pl.pallas_call(
        functools.partial(_proj_residual_kernel, n_in=len(ys)),
        grid=(m // tm,), in_specs=in_specs, out_specs=pl.BlockSpec((tm, d), lambda i: (i, 0)),
        out_shape=jax.ShapeDtypeStruct((m, d), F32),
        compiler_params=_cparams("parallel"), name="proj_residual")(x, *ys, *ws)


FFN_CHUNKS = 1


def _ffn_kernel(*refs, final_norm, n_mix):
    x_ref, g_ref, wg_ref, wu_ref, wd_ref = refs[:5]
    pos = 5
    if final_norm:
        gf_ref = refs[pos]
        pos += 1
    y_refs = refs[pos:pos + n_mix]
    wo_refs = refs[pos + n_mix:pos + 2 * n_mix]
    o_ref = refs[pos + 2 * n_mix]
    x = x_ref[...]
    for ym_ref, wo_ref in zip(y_refs, wo_refs):
        x = x + jnp.dot(ym_ref[...].astype(BF16), wo_ref[...], preferred_element_type=F32)
    xn = _rmsnorm(x, g_ref[...]).astype(BF16)
    fc = wg_ref.shape[1] // FFN_CHUNKS
    out = x
    for c in range(FFN_CHUNKS):
        cols = slice(c * fc, (c + 1) * fc)
        gate = jnp.dot(xn, wg_ref[:, cols], preferred_element_type=F32)
        up = jnp.dot(xn, wu_ref[:, cols], preferred_element_type=F32)
        h = (gate * _sigmoid(gate) * up).astype(BF16)
        out = out + jnp.dot(h, wd_ref[cols, :], preferred_element_type=F32)
    o_ref[...] = out
    if final_norm:
        refs[-1][...] = _rmsnorm(out, gf_ref[...])


def ffn_residual(x, ys, w_outs, g, wg, wu, wd, final_g=None):
    m, d = x.shape
    tm = _row_tile(m)
    final_norm = final_g is not None
    resident = lambda a: pl.BlockSpec(a.shape, lambda i: (0, 0), pipeline_mode=pl.Buffered(1))
    row = pl.BlockSpec((tm, d), lambda i: (i, 0))
    vec = pl.BlockSpec((1, d), lambda i: (0, 0))
    ins = [x, g.reshape(1, d), wg, wu, wd]
    in_specs = [row, vec, resident(wg), resident(wu), resident(wd)]
    out_shape = [jax.ShapeDtypeStruct((m, d), F32)]
    out_specs = [row]
    if final_norm:
        ins.append(final_g.reshape(1, d))
        in_specs.append(vec)
        out_shape.append(jax.ShapeDtypeStruct((m, d), F32))
        out_specs.append(row)
    ins += list(ys) + list(w_outs)
    in_specs += [pl.BlockSpec((tm, y.shape[1]), lambda i: (i, 0)) for y in ys]
    in_specs += [resident(w) for w in w_outs]
    res = pl.pallas_call(
        functools.partial(_ffn_kernel, final_norm=final_norm, n_mix=len(ys)),
        grid=(m // tm,), in_specs=in_specs, out_specs=out_specs, out_shape=out_shape,
        compiler_params=_cparams("parallel"), name="ffn")(*ins)
    return res if final_norm else res[0]


def _head_ones():
    r = _iota2((W_A, W_A), 0) // DH
    c = _iota2((W_A, W_A), 1) // DH
    return (r == c).astype(F32)


def _rwkv_pre_kernel(*refs, seq_blocks, has_prev):
    if has_prev:
        (pa_ref, prev_ref, mu_ref, wlora_ref, gup_ref, w0_ref, a0_ref, kk_ref, ka_ref,
         r_out, lw_out, k_out, v_out, kkn_out, a_out, g_out) = refs
    else:
        (pa_ref, mu_ref, wlora_ref, gup_ref, w0_ref, a0_ref, kk_ref, ka_ref,
         r_out, lw_out, k_out, v_out, kkn_out, a_out, g_out, carry) = refs
    pa = pa_ref[...]
    tm = pa.shape[0]
    if has_prev:
        shifted = prev_ref[...]
    else:
        i = pl.program_id(0)
        first = (i % seq_blocks) == 0
        prev_row = jnp.where(first, 0.0, carry[...])
        rolled = pltpu.roll(pa, 1, 0)
        shifted = jnp.where(_iota2(pa.shape, 0) == 0, prev_row, rolled)
        carry[...] = pa[tm - 1:tm, :]
    z = pa + (shifted - pa) * mu_ref[...]
    r = z[:, 0:W_A]
    k = z[:, W_A:2 * W_A]
    v = z[:, 2 * W_A:3 * W_A]
    lo = z[:, 3 * W_A:3 * W_A + LORA_W + LORA_A]
    gd = z[:, 3 * W_A + LORA_W + LORA_A:]
    lane = _iota2(lo.shape, 1)
    lo = jnp.where(lane < LORA_W, jnp.tanh(lo), lo)
    up = _bdot(lo, wlora_ref[...])
    w = -_softplus(-(w0_ref[...] + up[:, :W_A])) - 0.5
    a = _sigmoid(a0_ref[...] + up[:, W_A:])
    g = _bdot(_sigmoid(gd), gup_ref[...])
    kk = k * kk_ref[...]
    ss = _ones_dot_right(kk * kk, _head_ones())
    kk = kk / jnp.maximum(jnp.sqrt(ss), 1e-12)
    r_out[...] = r
    lw_out[...] = -jnp.exp(w)
    k_out[...] = k * (1.0 + (a - 1.0) * ka_ref[...])
    v_out[...] = v
    kkn_out[...] = kk
    a_out[...] = a
    g_out[...] = g


def rwkv_pre(pa, pa_prev, seq_len, p):
    m = pa.shape[0]
    has_prev = pa_prev is not None
    tm = m if has_prev else min(512, seq_len)
    vec = lambda a: a.reshape(1, -1)
    ins = [pa] + ([pa_prev] if has_prev else []) + [
        vec(p['mu']), p['w_lora'], p['g_up'], vec(p['w0']), vec(p['a0']), vec(p['k_k']), vec(p['k_a'])]
    row = lambda n: pl.BlockSpec((tm, n), lambda i: (i, 0))
    full = lambda a: pl.BlockSpec(a.shape, lambda i: (0, 0))
    in_specs = [row(A_COLS)] + ([row(A_COLS)] if has_prev else []) + [full(a) for a in ins[-7:]]
    scratch = [] if has_prev else [pltpu.VMEM((1, A_COLS), F32)]
    return pl.pallas_call(
        functools.partial(_rwkv_pre_kernel, seq_blocks=max(seq_len // tm, 1), has_prev=has_prev),
        grid=(m // tm,), in_specs=in_specs, out_specs=[row(W_A)] * 7,
        out_shape=[jax.ShapeDtypeStruct((m, W_A), F32)] * 7, scratch_shapes=scratch,
        compiler_params=_cparams("arbitrary"), name="rwkv_pre")(*ins)


def _group_norm_gate(y, r, k, v, g, gng, gnb, rk):
    outs = []
    for h in range(N_HEADS):
        sl = slice(h * DH, (h + 1) * DH)
        yh = y[:, sl]
        mean = jnp.mean(yh, axis=-1, keepdims=True)
        var = jnp.mean(jnp.square(yh - mean), axis=-1, keepdims=True)
        yn = (yh - mean) * lax.rsqrt(var + GN_EPS) * gng[:, sl] + gnb[:, sl]
        bonus = jnp.sum(r[:, sl] * k[:, sl] * rk[:, sl], axis=-1, keepdims=True) * v[:, sl]
        outs.append(yn + bonus)
    return jnp.concatenate(outs, axis=-1) * g


def _rwkv_chunk_kernel(r_ref, lw_ref, k_ref, v_ref, kk_ref, a_ref, g_ref, gng_ref, gnb_ref, rk_ref,
                       y_ref, s_ref, s_scr):
    c = pl.program_id(1)
    C = RWKV_CHUNK

    @pl.when(c == 0)
    def _():
        s_scr[...] = jnp.zeros_like(s_scr)

    tri = (_iota2((C, C), 1) <= _iota2((C, C), 0)).astype(F32)

    n = RWKV_GROUP * C
    gw = RWKV_GROUP * DH
    rb = _iota2((n, gw), 0)
    cb = _iota2((n, gw), 1)
    bdmask = (rb // C) == (cb // DH)
    rb2 = _iota2((n, n), 0) % C
    cb2 = _iota2((n, n), 1) % C
    strict = cb2 < rb2
    lower = cb2 <= rb2

    def bd(x):
        return jnp.where(bdmask, jnp.concatenate([x] * RWKV_GROUP, axis=0), 0.0)

    nb = r_ref.shape[0]
    ng = N_HEADS // RWKV_GROUP
    chains = [(bi, gi) for bi in range(nb) for gi in range(ng)]
    e_last, xar, ybk, vbd, s0 = {}, {}, {}, {}, {}
    for bi in range(nb):
        lw = lw_ref[bi]
        cum = _ones_dot_left(tri, lw)
        e_in = jnp.exp(cum)
        e_neg = jnp.exp(-cum)
        kk = kk_ref[bi]
        a_hat = -kk * jnp.exp(cum - lw)
        b_hat = kk * a_ref[bi] * e_neg
        k_hat = k_ref[bi] * e_neg
        r_hat = r_ref[bi] * e_in
        v = v_ref[bi]
        e_last[bi] = e_in[C - 1:C, :]
        for gi in range(ng):
            sl = slice(gi * gw, (gi + 1) * gw)
            xar[bi, gi] = jnp.concatenate([bd(a_hat[:, sl]), bd(r_hat[:, sl])], axis=0).astype(BF16)
            ybk[bi, gi] = jnp.concatenate([bd(b_hat[:, sl]), bd(k_hat[:, sl])], axis=0).astype(BF16)
            vbd[bi, gi] = bd(v[:, sl]).astype(BF16)
            s0[bi, gi] = s_scr[bi, gi]
    gm = {ch: _bdot_nt(xar[ch], ybk[ch]) for ch in chains}
    xs = {ch: _bdot_nt(xar[ch], s0[ch]) for ch in chains}
    pw = {ch: jnp.where(strict, gm[ch][:n, :n], 0.0) for ch in chains}
    u = {ch: xs[ch][:n] + _bdot(jnp.where(strict, gm[ch][:n, n:], 0.0), vbd[ch]) for ch in chains}
    u = {ch: u[ch] + _bdot(pw[ch], u[ch]) for ch in chains}
    for _ in range(int(math.log2(C)) - 1):
        pw = {ch: _bdot(pw[ch], pw[ch]) for ch in chains}
        u = {ch: u[ch] + _bdot(pw[ch], u[ch]) for ch in chains}
    uv = {ch: jnp.concatenate([u[ch].astype(BF16), vbd[ch]], axis=0) for ch in chains}
    ys = {}
    for ch in chains:
        m_r = jnp.concatenate([jnp.where(lower, gm[ch][n:, :n], 0.0), jnp.where(lower, gm[ch][n:, n:], 0.0)],
                              axis=1)
        y_bd = xs[ch][n:] + _bdot(m_r, uv[ch])
        ys[ch] = sum(y_bd[hh * C:(hh + 1) * C] for hh in range(RWKV_GROUP))
    for bi, gi in chains:
        s_scr[bi, gi] = ((s0[bi, gi] + _bdot_tn(uv[bi, gi], ybk[bi, gi]))
                         * e_last[bi][:, gi * gw:(gi + 1) * gw])
    for bi in range(nb):
        y = jnp.concatenate([ys[bi, gi] for gi in range(ng)], axis=-1)
        y_ref[bi] = _group_norm_gate(y, r_ref[bi], k_ref[bi], v_ref[bi], g_ref[bi], gng_ref[...],
                                     gnb_ref[...], rk_ref[...])

    @pl.when(c == pl.num_programs(1) - 1)
    def _():
        for bi in range(r_ref.shape[0]):
            for h in range(N_HEADS):
                hh = h % RWKV_GROUP
                s_ref[bi, h] = s_scr[bi, h // RWKV_GROUP][hh * DH:(hh + 1) * DH, hh * DH:(hh + 1) * DH]


def rwkv_chunked(pre, batch, seq_len, p):
    C = RWKV_CHUNK
    nc = seq_len // C
    nb = RWKV_SEQS if batch % RWKV_SEQS == 0 else 1
    seq = lambda a: a.reshape(batch, seq_len, W_A)
    blk = pl.BlockSpec((nb, C, W_A), lambda b, c: (b, c, 0))
    vec = lambda a: a.reshape(1, W_A)
    full = pl.BlockSpec((1, W_A), lambda b, c: (0, 0))
    y, s = pl.pallas_call(
        _rwkv_chunk_kernel, grid=(batch // nb, nc),
        in_specs=[blk] * 7 + [full] * 3,
        out_specs=[blk, pl.BlockSpec((nb, N_HEADS, DH, DH), lambda b, c: (b, 0, 0, 0))],
        out_shape=[jax.ShapeDtypeStruct((batch, seq_len, W_A), F32),
                   jax.ShapeDtypeStruct((batch, N_HEADS, DH, DH), F32)],
        scratch_shapes=[pltpu.VMEM((nb, N_HEADS // RWKV_GROUP, RWKV_GROUP * DH, RWKV_GROUP * DH), F32)],
        compiler_params=_cparams("parallel", "arbitrary"), name="rwkv_chunk",
    )(*[seq(a) for a in pre], vec(p['gn_g']), vec(p['gn_b']), vec(p['r_k']))
    return y.reshape(batch * seq_len, W_A), s


def _rwkv_step_kernel(r_ref, lw_ref, k_ref, v_ref, kk_ref, a_ref, g_ref, gng_ref, gnb_ref, rk_ref,
                      s_ref, y_ref, so_ref, vt_scr, yt_scr):
    tr = lambda ref: ref[...].T
    r, k, kk, g = tr(r_ref), tr(k_ref), tr(kk_ref), tr(g_ref)
    v = tr(v_ref)
    vt_scr[...] = v
    w = jnp.exp(tr(lw_ref))
    b = kk * tr(a_ref)
    outs = []
    for hh in range(2):
        sl = slice(hh * DH, (hh + 1) * DH)
        rh, kh, nkk, wh, bh = r[sl], k[sl], -kk[sl], w[sl], b[sl]

        def body(i, carry):
            s = s_ref[hh, i]
            sa = jnp.sum(s * nkk, axis=0, keepdims=True)
            s = s * wh + sa * bh + vt_scr[pl.ds(hh * DH + i, 1), :] * kh
            so_ref[hh, i] = s
            yt_scr[pl.ds(hh * DH + i, 1), :] = jnp.sum(s * rh, axis=0, keepdims=True)
            return carry

        lax.fori_loop(0, DH, body, 0)
        yh = yt_scr[sl, :]
        mean = jnp.mean(yh, axis=0, keepdims=True)
        var = jnp.mean(jnp.square(yh - mean), axis=0, keepdims=True)
        yn = (yh - mean) * lax.rsqrt(var + GN_EPS) * gng_ref[sl, :] + gnb_ref[sl, :]
        bonus = jnp.sum(rh * kh * rk_ref[sl, :], axis=0, keepdims=True) * v[sl]
        outs.append((yn + bonus) * g[sl])
    y_ref[...] = jnp.concatenate(outs, axis=0).T


def rwkv_step(pre, state_t, layer, p):
    batch = state_t.shape[-1]
    pair = 2 * DH
    blk = pl.BlockSpec((batch, pair), lambda h: (0, h))
    col = lambda a: a.reshape(W_A, 1)
    cblk = pl.BlockSpec((pair, 1), lambda h: (h, 0))
    y, s = pl.pallas_call(
        _rwkv_step_kernel, grid=(N_HEADS // 2,),
        in_specs=[blk] * 7 + [cblk] * 3 + [pl.BlockSpec((None, 2, DH, DH, batch), lambda h: (layer, h, 0, 0, 0))],
        out_specs=[blk, pl.BlockSpec((2, DH, DH, batch), lambda h: (h, 0, 0, 0))],
        out_shape=[jax.ShapeDtypeStruct((batch, W_A), F32),
                   jax.ShapeDtypeStruct((N_HEADS, DH, DH, batch), F32)],
        scratch_shapes=[pltpu.VMEM((pair, batch), F32), pltpu.VMEM((pair, batch), F32)],
        compiler_params=_cparams("parallel"), name="rwkv_step",
    )(*pre, col(p['gn_g']), col(p['gn_b']), col(p['r_k']), state_t)
    return y, s


def _fox_gate_kernel(f_ref, bf_ref, lf_ref, c_ref, ct_ref):
    t = f_ref.shape[0]
    lf = _log_sigmoid(f_ref[...] + bf_ref[...])
    lf_ref[...] = lf[:, :N_HEADS]
    blk = LANE
    row = _iota2((blk, blk), 0)
    col = _iota2((blk, blk), 1)
    tri = (col <= row).astype(F32)
    carry = jnp.zeros((1, LANE), F32)
    for i in range(t // blk):
        c_blk = _ones_dot_left(tri, lf[i * blk:(i + 1) * blk]) + carry
        c_ref[i * blk:(i + 1) * blk, :] = c_blk
        ct_ref[i] = c_blk.T[:N_HEADS]
        carry = c_blk[blk - 1:blk, :]


def fox_gates(f_pad, b_f, batch, seq_len):
    bf = jnp.zeros((1, LANE), F32).at[0, :N_HEADS].set(b_f)
    return pl.pallas_call(
        _fox_gate_kernel, grid=(batch,),
        in_specs=[pl.BlockSpec((None, seq_len, LANE), lambda b: (b, 0, 0)),
                  pl.BlockSpec((1, LANE), lambda b: (0, 0))],
        out_specs=[pl.BlockSpec((None, seq_len, N_HEADS), lambda b: (b, 0, 0)),
                   pl.BlockSpec((None, seq_len, LANE), lambda b: (b, 0, 0)),
                   pl.BlockSpec((None, seq_len // LANE, N_HEADS, LANE), lambda b: (b, 0, 0, 0))],
        out_shape=[jax.ShapeDtypeStruct((batch, seq_len, N_HEADS), F32),
                   jax.ShapeDtypeStruct((batch, seq_len, LANE), F32),
                   jax.ShapeDtypeStruct((batch, seq_len // LANE, N_HEADS, LANE), F32)],
        compiler_params=_cparams("parallel"), name="fox_gates",
    )(f_pad.reshape(batch, seq_len, LANE), bf)


def _fox_prompt_kernel(q_ref, k_ref, v_ref, c_ref, ct_ref, o_ref, kb_scr, vlo_scr, vhi_scr, *, pair_axis):
    pr = pl.program_id(pair_axis)
    qi = pl.program_id(2)
    tq = q_ref.shape[0]
    tk = tq
    lane = _iota2((1, LANE), 1)
    lo = lane < DH

    @pl.when(qi == 0)
    def _():
        kb_scr[...] = k_ref[...].astype(BF16)
        v = v_ref[...]
        vlo_scr[...] = jnp.where(lo, v, 0.0).astype(BF16)
        vhi_scr[...] = jnp.where(lo, 0.0, v).astype(BF16)

    q = q_ref[...] * QK_SCALE
    q_lo = jnp.where(lo, q, 0.0).astype(BF16)
    q_hi = jnp.where(lo, 0.0, q).astype(BF16)
    c_all = c_ref[...]
    head_lane = _iota2(c_all.shape, 1)
    cq = [jnp.sum(jnp.where(head_lane == 2 * pr + j, c_all, 0.0), axis=-1, keepdims=True)
          for j in range(2)]
    qs = (q_lo, q_hi)
    vs = (vlo_scr, vhi_scr)

    def step(j, carry, masked):
        m, l, acc = carry
        off = pl.multiple_of(j * tk, tk)
        kb = kb_scr[pl.ds(off, tk), :]
        two = range(2)
        ck = [jnp.concatenate([ct_ref[j * (tk // LANE) + i, pl.ds(2 * pr + hh, 1), :]
                               for i in range(tk // LANE)], axis=-1) for hh in two]
        s = [_bdot_nt(qs[hh], kb) + cq[hh] - ck[hh] for hh in two]
        if masked:
            causal = _iota2(s[0].shape, 1) <= _iota2(s[0].shape, 0)
            s = [jnp.where(causal, s[hh], NEG_INF) for hh in two]
        mn = [jnp.maximum(m[hh], jnp.max(s[hh], axis=-1, keepdims=True)) for hh in two]
        alphas = [jnp.exp(m[hh] - mn[hh]) for hh in two]
        pmat = [jnp.exp(s[hh] - mn[hh]) for hh in two]
        pvs = [jnp.dot(pmat[hh].astype(BF16), vs[hh][pl.ds(off, tk), :], preferred_element_type=F32)
               for hh in two]
        new_l = [l[hh] * alphas[hh] + jnp.sum(pmat[hh], axis=-1, keepdims=True) for hh in two]
        acc = acc * jnp.where(lo, alphas[0], alphas[1]) + pvs[0] + pvs[1]
        return tuple(mn), tuple(new_l), acc

    init = ((jnp.full((tq, 1), NEG_INF, F32),) * 2, (jnp.zeros((tq, 1), F32),) * 2,
            jnp.zeros((tq, LANE), F32))
    carry = lax.fori_loop(0, qi, lambda j, cr: step(j, cr, False), init)
    m, l, acc = step(qi, carry, True)
    o_ref[...] = acc / jnp.where(lo, l[0], l[1])


def fox_prompt(q, k, v, c, ct, batch, seq_len):
    tq = min(ATT_BLOCK, seq_len)
    seq = lambda a: a.reshape(-1, batch, seq_len, W_A)
    last = lambda a: 0 if a.ndim == 2 else a.shape[0] - 1
    kv_spec = lambda a: pl.BlockSpec((None, None, seq_len, LANE),
                                     lambda b, pr, i, ly=last(a): (ly, b, 0, pr))
    o = pl.pallas_call(
        functools.partial(_fox_prompt_kernel, pair_axis=1),
        grid=(batch, N_HEADS // 2, seq_len // tq),
        in_specs=[pl.BlockSpec((None, None, tq, LANE), lambda b, pr, i: (0, b, i, pr)), kv_spec(k), kv_spec(v),
                  pl.BlockSpec((None, tq, LANE), lambda b, pr, i: (b, i, 0)),
                  pl.BlockSpec((None, seq_len // LANE, N_HEADS, LANE), lambda b, pr, i: (b, 0, 0, 0))],
        out_specs=pl.BlockSpec((None, tq, LANE), lambda b, pr, i: (b, i, pr)),
        out_shape=jax.ShapeDtypeStruct((batch, seq_len, W_A), F32),
        scratch_shapes=[pltpu.VMEM((seq_len, LANE), BF16)] * 3,
        compiler_params=_cparams("parallel", "parallel", "arbitrary"), name="fox_prompt",
    )(seq(q), seq(k), seq(v), c, ct)
    return o.reshape(batch * seq_len, W_A)


FOX_PAGES_PER_STEP = 16


def _fox_decode_kernel(pt_ref, q_ref, kn_ref, vn_ref, lfn_ref, *refs, n_pages):
    g = n_pages
    lf_refs = refs[0:g]
    k_refs = refs[g:2 * g]
    v_refs = refs[2 * g:3 * g]
    o_ref = refs[3 * g]
    qb_scr, m_scr, l_scr, acc_scr, tail_scr = refs[3 * g + 1:]
    step = pl.program_id(1)
    page = k_refs[0].shape[-1]
    eye = _iota2((DH, DH), 0) == _iota2((DH, DH), 1)

    @pl.when(step == 0)
    def _():
        m_scr[...] = jnp.full_like(m_scr, NEG_INF)
        l_scr[...] = jnp.zeros_like(l_scr)
        acc_scr[...] = jnp.zeros_like(acc_scr)
        tail_scr[...] = jnp.zeros_like(tail_scr)
        q = q_ref[...] * QK_SCALE
        for h in range(N_HEADS):
            q_col = jnp.sum(jnp.where(eye, q[:, h * DH:(h + 1) * DH], 0.0), axis=-1, keepdims=True)
            qb_scr[h] = jnp.broadcast_to(q_col, (DH, page))

    lfn = lfn_ref[...]
    after = (_iota2((page, page), 0) > _iota2((page, page), 1)).astype(F32)
    tail = tail_scr[...]
    decs = []
    for i in range(g):
        lf = lf_refs[i][...]
        decs.append(_ones_dot_right(lf, after) + tail + lfn)
        tail = tail + jnp.sum(lf, axis=-1, keepdims=True)
    tail_scr[...] = tail
    dec = jnp.concatenate(decs, axis=-1)
    for h in range(N_HEADS):
        qb = qb_scr[h]
        s = jnp.concatenate([jnp.sum(qb * k_refs[i][h], axis=0, keepdims=True) for i in range(g)],
                            axis=-1) + dec[h:h + 1, :]
        m_old = m_scr[h][:, :1]
        mn = jnp.maximum(m_old, jnp.max(s, axis=-1, keepdims=True))
        alpha = jnp.exp(m_old - mn)
        pmat = jnp.exp(s - mn)
        l_new = l_scr[h][:, :1] * alpha + jnp.sum(pmat, axis=-1, keepdims=True)
        m_scr[h] = jnp.broadcast_to(mn, (1, LANE))
        l_scr[h] = jnp.broadcast_to(l_new, (1, LANE))
        acc = acc_scr[h] * alpha
        for i in range(g):
            acc = acc + pmat[:, i * page:(i + 1) * page] * v_refs[i][h]
        acc_scr[h] = acc

    @pl.when(step == pl.num_programs(1) - 1)
    def _():
        q = q_ref[...] * QK_SCALE
        kn = kn_ref[...]
        vn = vn_ref[...]
        outs = []
        for h in range(N_HEADS):
            sl = slice(h * DH, (h + 1) * DH)
            s_new = jnp.sum(q[:, sl] * kn[:, sl], axis=-1, keepdims=True)
            m_old = m_scr[h][:, :1]
            mn = jnp.maximum(m_old, s_new)
            alpha = jnp.exp(m_old - mn)
            p_new = jnp.exp(s_new - mn)
            lt = l_scr[h][:, :1] * alpha + p_new
            o_col = jnp.sum(acc_scr[h], axis=-1, keepdims=True) * alpha
            o_row = jnp.sum(jnp.where(eye, o_col, 0.0), axis=0, keepdims=True)
            outs.append((o_row + p_new * vn[:, sl]) / lt)
        o_ref[...] = jnp.concatenate(outs, axis=-1)


def fox_decode(q, k_new, v_new, logf_new, cache_kt, cache_vt, cache_lf_t, layer, page_table):
    batch, n_pages = page_table.shape
    page = cache_kt.shape[-1]
    g = min(FOX_PAGES_PER_STEP, n_pages)
    steps = n_pages // g
    tok = lambda a: a.reshape(batch, 1, W_A)
    tok_spec = pl.BlockSpec((None, 1, W_A), lambda b, s, pt: (b, 0, 0))

    def page_idx(b, s, pt, i):
        return pt[b * n_pages + (n_pages - 1 - (s * g + i))]

    lf_specs = [pl.BlockSpec((None, None, N_HEADS, page),
                             lambda b, s, pt, i=i: (layer, page_idx(b, s, pt, i), 0, 0)) for i in range(g)]
    kv_specs = [pl.BlockSpec((None, None, N_HEADS, DH, page),
                             lambda b, s, pt, i=i: (layer, page_idx(b, s, pt, i), 0, 0, 0)) for i in range(g)]
    grid_spec = pltpu.PrefetchScalarGridSpec(
        num_scalar_prefetch=1, grid=(batch, steps),
        in_specs=[tok_spec, tok_spec, tok_spec,
                  pl.BlockSpec((None, N_HEADS, 1), lambda b, s, pt: (b, 0, 0))] + lf_specs + kv_specs + kv_specs,
        out_specs=tok_spec,
        scratch_shapes=[pltpu.VMEM((N_HEADS, DH, page), F32), pltpu.VMEM((N_HEADS, 1, LANE), F32),
                        pltpu.VMEM((N_HEADS, 1, LANE), F32), pltpu.VMEM((N_HEADS, DH, page), F32),
                        pltpu.VMEM((N_HEADS, 1), F32)])
    o = pl.pallas_call(
        functools.partial(_fox_decode_kernel, n_pages=g), grid_spec=grid_spec,
        out_shape=jax.ShapeDtypeStruct((batch, 1, W_A), F32),
        compiler_params=_cparams("parallel", "arbitrary"), name="fox_decode",
    )(page_table.reshape(-1), tok(q), tok(k_new), tok(v_new), logf_new.reshape(batch, N_HEADS, 1),
      *([cache_lf_t] * g), *([cache_kt] * g), *([cache_vt] * g))
    return o.reshape(batch, W_A)


def _t5_buckets(dist):
    n = np.maximum(dist, 0)
    max_exact = N_BUCKETS // 2
    nf = np.maximum(n, 1).astype(np.float32)
    large = max_exact + (np.log(nf / max_exact) / math.log(MAX_DISTANCE / max_exact)
                         * (N_BUCKETS - max_exact)).astype(np.int32)
    large = np.minimum(large, N_BUCKETS - 1)
    return np.where(n < max_exact, n, large).astype(np.int32)


def _bias_gather_kernel(idx_ref, table_ref, o_ref):
    col = pl.program_id(0)
    idx = idx_ref[...]
    out = jnp.zeros(idx.shape, F32)
    for bkt in range(N_BUCKETS):
        out = jnp.where(idx == bkt, table_ref[bkt, col], out)
    o_ref[...] = out


def bias_gather(idx, table):
    r, c = idx.shape
    ncol = table.shape[1]
    return pl.pallas_call(
        _bias_gather_kernel, grid=(ncol,),
        in_specs=[pl.BlockSpec((r, c), lambda j: (0, 0)),
                  pl.BlockSpec(memory_space=pltpu.SMEM)],
        out_specs=pl.BlockSpec((None, r, c), lambda j: (j, 0, 0)),
        out_shape=jax.ShapeDtypeStruct((ncol, r, c), F32),
        compiler_params=_cparams("parallel"), name="bias_gather")(idx, table)


def _diff_lambda(lq1, lk1, lq2, lk2, lambda_init):
    return (jnp.exp(jnp.sum(lq1 * lk1, axis=-1, keepdims=True))
            - jnp.exp(jnp.sum(lq2 * lk2, axis=-1, keepdims=True)) + lambda_init)


def _diff_prompt_kernel(q_ref, k_ref, v_ref, bias_ref, far_ref, lam_ref, sub_ref, o_ref, kb_scr, vb_scr,
                        *, lambda_init):
    qi = pl.program_id(2)
    tq = q_ref.shape[0]
    tk = tq
    lane = _iota2((1, LANE), 1)
    lo = lane < DH

    @pl.when(qi == 0)
    def _():
        kb_scr[...] = k_ref[...].astype(BF16)
        vb_scr[...] = v_ref[...].astype(BF16)

    q = q_ref[...] * QK_SCALE
    qs = (jnp.where(lo, q, 0.0).astype(BF16), jnp.where(lo, 0.0, q).astype(BF16))
    head = pl.program_id(1)
    fars = (far_ref[head, 0], far_ref[head, 1])

    def step(j, carry, kind):
        m, l, acc = carry
        off = pl.multiple_of(j * tk, tk)
        kb = kb_scr[pl.ds(off, tk), :]
        vb = vb_scr[pl.ds(off, tk), :]
        two = range(2)
        s = [_bdot_nt(qs[mp], kb) for mp in two]
        if kind == 'far':
            s = [s[mp] + fars[mp] for mp in two]
        elif kind == 'near':
            s = [s[mp] + bias_ref[mp, 1] for mp in two]
        else:
            causal = _iota2(s[0].shape, 1) <= _iota2(s[0].shape, 0)
            s = [jnp.where(causal, s[mp] + bias_ref[mp, 0], NEG_INF) for mp in two]
        mn = [jnp.maximum(m[mp], jnp.max(s[mp], axis=-1, keepdims=True)) for mp in two]
        alpha = [jnp.exp(m[mp] - mn[mp]) for mp in two]
        pmat = [jnp.exp(s[mp] - mn[mp]) for mp in two]
        pv = [jnp.dot(pmat[mp].astype(BF16), vb, preferred_element_type=F32) for mp in two]
        new_l = [l[mp] * alpha[mp] + jnp.sum(pmat[mp], axis=-1, keepdims=True) for mp in two]
        new_acc = [acc[mp] * alpha[mp] + pv[mp] for mp in two]
        return tuple(mn), tuple(new_l), tuple(new_acc)

    init = ((jnp.full((tq, 1), NEG_INF, F32),) * 2, (jnp.zeros((tq, 1), F32),) * 2,
            (jnp.zeros((tq, LANE), F32),) * 2)
    carry = lax.fori_loop(0, jnp.maximum(qi - 1, 0), lambda j, cr: step(j, cr, 'far'), init)
    carry = lax.cond(qi >= 1, lambda cr: step(qi - 1, cr, 'near'), lambda cr: cr, carry)
    m, l, acc = step(qi, carry, 'diag')
    lam = _diff_lambda(lam_ref[0:1, :], lam_ref[1:2, :], lam_ref[2:3, :], lam_ref[3:4, :], lambda_init)
    o = acc[0] / l[0] - lam * (acc[1] / l[1])
    o_ref[...] = _rmsnorm(o, sub_ref[...]) * (1.0 - lambda_init)


def diff_prompt(q, k, v, bias_tiles, far_bias, lam_vecs, subln_g, lambda_init, batch, seq_len):
    tq = min(ATT_BLOCK, seq_len)
    seq = lambda a: a.reshape(-1, batch, seq_len, W_C)
    last = lambda a: 0 if a.ndim == 2 else a.shape[0] - 1
    kv_spec = lambda a: pl.BlockSpec((None, None, seq_len, LANE),
                                     lambda b, h, i, ly=last(a): (ly, b, 0, h))
    blk = pl.BlockSpec((None, tq, LANE), lambda b, h, i: (b, i, h))
    o = pl.pallas_call(
        functools.partial(_diff_prompt_kernel, lambda_init=lambda_init),
        grid=(batch, N_HEADS, seq_len // tq),
        in_specs=[pl.BlockSpec((None, None, tq, LANE), lambda b, h, i: (0, b, i, h)), kv_spec(k), kv_spec(v),
                  pl.BlockSpec((None, 2, 2, tq, tq), lambda b, h, i: (h, 0, 0, 0, 0)),
                  pl.BlockSpec(memory_space=pltpu.SMEM),
                  pl.BlockSpec((4, DH), lambda b, h, i: (0, 0)),
                  pl.BlockSpec((1, LANE), lambda b, h, i: (0, 0))],
        out_specs=blk,
        out_shape=jax.ShapeDtypeStruct((batch, seq_len, W_C), F32),
        scratch_shapes=[pltpu.VMEM((seq_len, LANE), BF16)] * 2,
        compiler_params=_cparams("parallel", "parallel", "arbitrary"), name="diff_prompt",
    )(seq(q), seq(k), seq(v), bias_tiles, far_bias, lam_vecs, subln_g.reshape(1, LANE))
    return o.reshape(batch * seq_len, W_C)


DIFF_PAGES_PER_STEP = 8


def _diff_decode_kernel(pt_ref, q_ref, kn_ref, vn_ref, bias_ref, rel_ref, lam_ref, sub_ref, *refs,
                        n_pages, lambda_init):
    g = n_pages
    k_refs = refs[0:g]
    v_refs = refs[g:2 * g]
    o_ref = refs[2 * g]
    m_scr, l_scr, acc_scr = refs[2 * g + 1:]
    step = pl.program_id(1)

    @pl.when(step == 0)
    def _():
        m_scr[...] = jnp.full_like(m_scr, NEG_INF)
        l_scr[...] = jnp.zeros_like(l_scr)
        acc_scr[...] = jnp.zeros_like(acc_scr)

    row = _iota2((N_HEADS, LANE), 0)
    lane = _iota2((N_HEADS, LANE), 1)
    qsel = ((row == 0) & (lane < DH)) | ((row == 1) & (lane >= DH))

    def q_tile(h):
        return jnp.where(qsel, q_ref[h:h + 1, :] * QK_SCALE, 0.0)

    page = k_refs[0].shape[0] // N_HEADS
    heads = range(N_HEADS)

    def head_rows(refs_, h):
        return jnp.concatenate([r_[pl.ds(h, page, stride=N_HEADS), :] for r_ in refs_], axis=0)

    s = [lax.dot_general(q_tile(h), head_rows(k_refs, h), (((1,), (1,)), ((), ())),
                         preferred_element_type=F32) + bias_ref[h] for h in heads]
    m_old = [m_scr[h][:, :1] for h in heads]
    mn = [jnp.maximum(m_old[h], jnp.max(s[h], axis=-1, keepdims=True)) for h in heads]
    alpha = [jnp.exp(m_old[h] - mn[h]) for h in heads]
    pmat = [jnp.exp(s[h] - mn[h]) for h in heads]
    pv = [jnp.dot(pmat[h], head_rows(v_refs, h), preferred_element_type=F32) for h in heads]
    for h in heads:
        l_new = l_scr[h][:, :1] * alpha[h] + jnp.sum(pmat[h], axis=-1, keepdims=True)
        m_scr[h] = jnp.broadcast_to(mn[h], (N_HEADS, LANE))
        l_scr[h] = jnp.broadcast_to(l_new, (N_HEADS, LANE))
        acc_scr[h] = acc_scr[h] * alpha[h] + pv[h]

    @pl.when(step == pl.num_programs(1) - 1)
    def _():
        lam = _diff_lambda(lam_ref[0:1, :], lam_ref[1:2, :], lam_ref[2:3, :], lam_ref[3:4, :], lambda_init)
        for h in range(N_HEADS):
            b_new = jnp.where(row[:, :1] == 0, rel_ref[0, 2 * h], rel_ref[0, 2 * h + 1])
            s_new = jnp.sum(q_tile(h) * kn_ref[h:h + 1, :], axis=-1, keepdims=True) + b_new
            m_old = m_scr[h][:, :1]
            mn = jnp.maximum(m_old, s_new)
            alpha = jnp.exp(m_old - mn)
            p_new = jnp.exp(s_new - mn)
            lt = l_scr[h][:, :1] * alpha + p_new
            o = (acc_scr[h] * alpha + p_new * vn_ref[h:h + 1, :]) / lt
            od = o[0:1] - lam * o[1:2]
            od = od * lax.rsqrt(jnp.mean(od * od, axis=-1, keepdims=True) + RMS_EPS)
            o_ref[:, h * LANE:(h + 1) * LANE] = od * sub_ref[...] * (1.0 - lambda_init)


def diff_decode(q, k_new, v_new, cache_k, cache_v, layer, page_table, page_bias, rel_bias, lam_vecs,
                subln_g, lambda_init):
    batch, n_pages = page_table.shape
    page = cache_k.shape[2]
    g = min(DIFF_PAGES_PER_STEP, n_pages)
    steps = n_pages // g
    heads = lambda a: a.reshape(batch, N_HEADS, LANE)
    head_spec = pl.BlockSpec((None, N_HEADS, LANE), lambda b, s, pt: (b, 0, 0))
    pool = cache_k.shape[1]
    cache_k = cache_k.reshape(-1, pool, page * N_HEADS, LANE)
    cache_v = cache_v.reshape(-1, pool, page * N_HEADS, LANE)
    kv_specs = [pl.BlockSpec((None, None, page * N_HEADS, LANE),
                             lambda b, s, pt, i=i: (layer, pt[b * n_pages + s * g + i], 0, 0))
                for i in range(g)]
    grid_spec = pltpu.PrefetchScalarGridSpec(
        num_scalar_prefetch=1, grid=(batch, steps),
        in_specs=[head_spec, head_spec, head_spec,
                  pl.BlockSpec((None, N_HEADS, N_HEADS, g * page), lambda b, s, pt: (s, 0, 0, 0)),
                  pl.BlockSpec(memory_space=pltpu.SMEM),
                  pl.BlockSpec((4, DH), lambda b, s, pt: (0, 0)),
                  pl.BlockSpec((1, LANE), lambda b, s, pt: (0, 0))] + kv_specs + kv_specs,
        out_specs=pl.BlockSpec((None, 1, W_C), lambda b, s, pt: (b, 0, 0)),
        scratch_shapes=[pltpu.VMEM((N_HEADS, N_HEADS, LANE), F32)] * 3)
    o = pl.pallas_call(
        functools.partial(_diff_decode_kernel, n_pages=g, lambda_init=lambda_init), grid_spec=grid_spec,
        out_shape=jax.ShapeDtypeStruct((batch, 1, W_C), F32),
        compiler_params=_cparams("parallel", "arbitrary"), name="diff_decode",
    )(page_table.reshape(-1), heads(q), heads(k_new), heads(v_new), page_bias, rel_bias, lam_vecs,
      subln_g.reshape(1, LANE), *([cache_k] * g), *([cache_v] * g))
    return o.reshape(batch, W_C)


def _prep_params(norm_mix, norm_ffn, norm_final, w_in_ab, w_out_ab, rwkv_mu, rwkv_w0, rwkv_w_up, rwkv_a0,
                 rwkv_a_up, rwkv_g_up, rwkv_k_k, rwkv_k_a, rwkv_r_k, rwkv_gn_g, rwkv_gn_b, fox_b_f, w_in_c,
                 w_out_c, diff_lq1, diff_lk1, diff_lq2, diff_lk2, diff_subln, rel_bias, ffn_gate, ffn_up,
                 ffn_down):
    depth = norm_mix.shape[0]
    o_r, o_wd, o_k, o_v, o_ad, o_gd = np.cumsum([0, W_A, LORA_W, W_A, W_A, LORA_A]).tolist()
    perm = np.concatenate([np.arange(o_r, o_r + W_A), np.arange(o_k, o_k + W_A), np.arange(o_v, o_v + W_A),
                           np.arange(o_wd, o_wd + LORA_W), np.arange(o_ad, o_ad + LORA_A),
                           np.arange(o_gd, o_gd + LORA_G)])
    layers = []
    for li in range(depth):
        j = li // 2
        lp = {'norm_mix': norm_mix[li], 'norm_ffn': norm_ffn[li],
              'ffn_gate': ffn_gate[li].astype(BF16), 'ffn_up': ffn_up[li].astype(BF16),
              'ffn_down': ffn_down[li].astype(BF16)}
        if li % 2 == 0:
            w = w_in_ab[j]
            wa = w[:, :A_COLS][:, perm]
            wb = w[:, A_COLS:]
            wf = jnp.pad(wb[:, 3 * W_A:], ((0, 0), (0, LANE - N_HEADS)))
            lp['w_in'] = [wa.astype(BF16), wb[:, :W_A].astype(BF16), wb[:, W_A:2 * W_A].astype(BF16),
                          wb[:, 2 * W_A:3 * W_A].astype(BF16), wf.astype(BF16)]
            lp['w_out'] = [w_out_ab[j][:W_A].astype(BF16), w_out_ab[j][W_A:].astype(BF16)]
            zeros = jnp.zeros((LORA_W, W_A), F32)
            w_lora = jnp.concatenate([jnp.concatenate([rwkv_w_up[j], zeros], axis=1),
                                      jnp.concatenate([zeros, rwkv_a_up[j]], axis=1)], axis=0)
            lp['rwkv'] = {'mu': rwkv_mu[j][perm], 'w_lora': w_lora.astype(BF16),
                          'g_up': rwkv_g_up[j].astype(BF16), 'w0': rwkv_w0[j], 'a0': rwkv_a0[j],
                          'k_k': rwkv_k_k[j], 'k_a': rwkv_k_a[j], 'r_k': rwkv_r_k[j].reshape(-1),
                          'gn_g': rwkv_gn_g[j], 'gn_b': rwkv_gn_b[j]}
            lp['fox_b_f'] = fox_b_f[j]
        else:
            w = w_in_c[j]
            lp['w_in'] = [w[:, :W_C].astype(BF16), w[:, W_C:2 * W_C].astype(BF16), w[:, 2 * W_C:].astype(BF16)]
            lp['w_out'] = [w_out_c[j].astype(BF16)]
            lp['lam_vecs'] = jnp.stack([diff_lq1[j], diff_lk1[j], diff_lq2[j], diff_lk2[j]])
            lp['subln'] = diff_subln[j]
            lp['lambda_init'] = 0.8 - 0.6 * math.exp(-0.3 * li)
        layers.append(lp)
    return layers, norm_final, rel_bias


def _prompt_bias(rel_bias, seq_len):
    tq = min(ATT_BLOCK, seq_len)
    d = np.arange(tq)[:, None] - np.arange(tq)[None, :]
    idx = np.concatenate([_t5_buckets(d), _t5_buckets(d + tq)], axis=0)
    tiles = bias_gather(jnp.asarray(idx), rel_bias)
    tiles = tiles.reshape(N_HEADS, 2, 2, tq, tq)
    far_idx = np.full((8, LANE), int(_t5_buckets(np.array([2 * tq]))[0]), np.int32)
    far = bias_gather(jnp.asarray(far_idx), rel_bias)[:, 0, 0].reshape(N_HEADS, 2)
    return tiles, far


def _decode_bias(rel_bias, n_pages, page, g):
    t0 = n_pages * page
    steps = n_pages // g
    rows = -(-steps // 8) * 8
    idx = np.zeros((rows, g * page), np.int32)
    idx[:steps] = _t5_buckets(t0 - np.arange(t0).reshape(steps, g * page))
    bias = bias_gather(jnp.asarray(idx), rel_bias)[:, :steps]
    bias = jnp.transpose(bias.reshape(N_HEADS, 2, steps, g * page), (2, 0, 1, 3))
    return jnp.pad(bias, ((0, 0), (0, 0), (0, N_HEADS - 2), (0, 0)))


def _decode_logf_kernel(f_ref, bf_ref, o_ref):
    o_ref[...] = _log_sigmoid(f_ref[...] + bf_ref[...])[:, :N_HEADS]


def _decode_logf(f_pad, b_f):
    bf = jnp.zeros((1, LANE), F32).at[0, :N_HEADS].set(b_f)
    return pl.pallas_call(
        _decode_logf_kernel, out_shape=jax.ShapeDtypeStruct((f_pad.shape[0], N_HEADS), F32),
        name="decode_logf")(f_pad, bf)


def _trunk(x, layers, norm_final, batch, seq_len, past, prompt_bias=None):
    fk, fv, fl, dk, dv, rs, sh = [], [], [], [], [], [], []
    y = None
    stack_kv = past is None and len(layers) == 4
    for li, lp in enumerate(layers):
        j = li // 2
        if li % 2 == 0:
            stack = {2: fk[0], 3: fv[0]} if (stack_kv and j == 1) else None
            pa, q, k, v, f_pad, xn = norm_proj(x, lp['norm_mix'], lp['w_in'], emit_xn=True, stack_prev=stack)
            if past is None:
                pre = rwkv_pre(pa, None, seq_len, lp['rwkv'])
                ya, s_fin = rwkv_chunked(pre, batch, seq_len, lp['rwkv'])
                logf, c, ct = fox_gates(f_pad, lp['fox_b_f'], batch, seq_len)
                yb = fox_prompt(q, k, v, c, ct, batch, seq_len)
            else:
                (pa_prev,) = norm_proj(past['state_shift'][j], None, lp['w_in'][:1])
                pre = rwkv_pre(pa, pa_prev, seq_len, lp['rwkv'])
                ya, s_fin = rwkv_step(pre, past['state_rwkv_t'], j, lp['rwkv'])
                logf = _decode_logf(f_pad, lp['fox_b_f'])
                yb = fox_decode(q, k, v, logf, past['cache_fox_kt'], past['cache_fox_vt'],
                                past['cache_fox_logf_t'], j, past['page_table'])
                logf = logf.reshape(batch, 1, N_HEADS)
            mix = [ya, yb]
            fk.append(k)
            fv.append(v)
            fl.append(logf)
            rs.append(s_fin)
            sh.append(xn.reshape(batch, seq_len, D_MODEL)[:, -1])
        else:
            stack = {1: dk[0], 2: dv[0]} if (stack_kv and j == 1) else None
            q, k, v = norm_proj(x, lp['norm_mix'], lp['w_in'], stack_prev=stack)
            if past is None:
                yc = diff_prompt(q, k, v, prompt_bias[0], prompt_bias[1], lp['lam_vecs'],
                                 lp['subln'], lp['lambda_init'], batch, seq_len)
            else:
                yc = diff_decode(q, k, v, past['cache_diff_k'], past['cache_diff_v'], j, past['page_table'],
                                 past['page_bias'], past['rel_bias'], lp['lam_vecs'], lp['subln'],
                                 lp['lambda_init'])
            mix = [yc]
            dk.append(k)
            dv.append(v)
        if li == len(layers) - 1:
            x, y = ffn_residual(x, mix, lp['w_out'], lp['norm_ffn'], lp['ffn_gate'], lp['ffn_up'],
                                lp['ffn_down'], norm_final)
        else:
            x = ffn_residual(x, mix, lp['w_out'], lp['norm_ffn'], lp['ffn_gate'], lp['ffn_up'], lp['ffn_down'])
    rs = jnp.stack(rs)
    if past is not None:
        rs = jnp.transpose(rs, (0, 4, 1, 2, 3))
    stacked = (lambda lst: lst[-1]) if stack_kv else jnp.stack
    heads = lambda a, w: a.reshape(a.shape[0], batch, seq_len, N_HEADS, w)
    return (y.reshape(batch, seq_len, D_MODEL), heads(stacked(fk), DH), heads(stacked(fv), DH), jnp.stack(fl),
            heads(stacked(dk), 2 * DH), heads(stacked(dv), 2 * DH), rs, jnp.stack(sh))


def kernel(x_prompt, x_sample, cache_fox_k, cache_fox_v, cache_fox_logf, cache_diff_k, cache_diff_v, state_rwkv, state_shift, page_table, norm_mix, norm_ffn, norm_final, w_in_ab, w_out_ab, rwkv_mu, rwkv_w0, rwkv_w_up, rwkv_a0, rwkv_a_up, rwkv_g_up, rwkv_k_k, rwkv_k_a, rwkv_r_k, rwkv_gn_g, rwkv_gn_b, fox_b_f, w_in_c, w_out_c, diff_lq1, diff_lk1, diff_lq2, diff_lk2, diff_subln, rel_bias, ffn_gate, ffn_up, ffn_down):
    layers, norm_final, rel_bias = _prep_params(
        norm_mix, norm_ffn, norm_final, w_in_ab, w_out_ab, rwkv_mu, rwkv_w0, rwkv_w_up, rwkv_a0, rwkv_a_up,
        rwkv_g_up, rwkv_k_k, rwkv_k_a, rwkv_r_k, rwkv_gn_g, rwkv_gn_b, fox_b_f, w_in_c, w_out_c, diff_lq1,
        diff_lk1, diff_lq2, diff_lk2, diff_subln, rel_bias, ffn_gate, ffn_up, ffn_down)
    bp, tp, d = x_prompt.shape
    bs, ts, _ = x_sample.shape
    n_pages = page_table.shape[1]
    page = cache_fox_k.shape[2]
    past = {
        'state_shift': state_shift, 'page_table': page_table,
        'state_rwkv_t': jnp.transpose(state_rwkv, (0, 2, 3, 4, 1)),
        'cache_fox_kt': jnp.transpose(cache_fox_k, (0, 1, 3, 4, 2)),
        'cache_fox_vt': jnp.transpose(cache_fox_v, (0, 1, 3, 4, 2)),
        'cache_fox_logf_t': jnp.swapaxes(cache_fox_logf, 2, 3),
        'cache_diff_k': cache_diff_k, 'cache_diff_v': cache_diff_v, 'rel_bias': rel_bias,
        'page_bias': _decode_bias(rel_bias, n_pages, page, min(DIFF_PAGES_PER_STEP, n_pages)),
    }
    out_p = _trunk(x_prompt.reshape(bp * tp, d), layers, norm_final, bp, tp, None,
                   _prompt_bias(rel_bias, tp))
    out_s = _trunk(x_sample.reshape(bs * ts, d), layers, norm_final, bs, ts, past)
    return (out_p[0], out_s[0]) + out_p[1:] + out_s[1:]
```
